```python
import jax, jax.numpy as jnp
from jax import lax
import numpy as np

D_MODEL = 1024
BATCH = 2
SEQ = 8192
DEPTH = 2
DEC_BATCH = 128
DEC_SEQ = 1
PAST_LEN = 8192
PAGE_SIZE = 128

HEAD_DIM = 64
A_GROUPS = ((128, 1), (512, 4), (2048, 16))
N_GROUPS_A = len(A_GROUPS)
H_A = D_MODEL // HEAD_DIM
HQ_B = D_MODEL // HEAD_DIM
HKV_B = 2
WINDOW_B = 128
BLOCK = 128
D_FF_DENSE = 2816
N_EXPERTS = 8
TOP_K = 2
D_FF_EXPERT = 1408
RMS_EPS = 1e-6
N_EVEN_LAYERS = (DEPTH + 1) // 2
N_ODD_LAYERS = DEPTH // 2

kernel_name = 'hybrid_dilated_swa_sink_decoder_step'


def rms_norm(x, g):
    xf = x.astype(jnp.float32)
    y = xf * lax.rsqrt(jnp.mean(xf * xf, axis=-1, keepdims=True) + RMS_EPS)
    return (y * g.astype(jnp.float32)).astype(x.dtype)


def alibi_slopes(n_heads):
    return jnp.asarray(2.0 ** (-8.0 * np.arange(1, n_heads + 1) / n_heads), dtype=jnp.float32)


def softmax_with_sink(s, sink):
    m = jnp.max(s, axis=-1)
    if sink is not None:
        m = jnp.maximum(m, sink)
    p = jnp.exp(s - m[..., None])
    denom = jnp.sum(p, axis=-1)
    if sink is not None:
        denom = denom + jnp.exp(sink - m)
    return p / denom[..., None], m + jnp.log(denom)


def banded_window_attention(q, k, v, step, n_steps, slopes, sink):
    N, L, Hq, hd = q.shape
    Hk = k.shape[2]
    G = Hq // Hk
    nb = -(-L // BLOCK)
    pad = nb * BLOCK - L
    qb = jnp.pad(q, ((0, 0), (0, pad), (0, 0), (0, 0))).reshape(N, nb, BLOCK, Hk, G, hd)

    def key_windows(a):
        a = jnp.pad(a, ((0, 0), (BLOCK, pad), (0, 0), (0, 0))).reshape(N, nb + 1, BLOCK, Hk, hd)
        return jnp.concatenate([a[:, :-1], a[:, 1:]], axis=2)

    kw, vw = key_windows(k), key_windows(v)
    s = jnp.einsum('ncqkgd,ncskd->nckgqs', qb, kw, preferred_element_type=jnp.float32) * (hd ** -0.5)
    qi = jnp.arange(BLOCK)[:, None]
    sj = jnp.arange(2 * BLOCK)[None, :]
    dist = qi - sj + BLOCK
    in_band = (dist >= 0) & (dist <= n_steps)
    key_row = (jnp.arange(nb) * BLOCK - BLOCK)[:, None, None] + sj[None]
    mask = in_band[None] & (key_row >= 0)
    bias = -(slopes.reshape(Hk, G, 1, 1) * (step * dist).astype(jnp.float32))
    s = jnp.where(mask[None, :, None, None], s + bias, -jnp.inf)
    sink_r = None if sink is None else sink.astype(jnp.float32).reshape(Hk, G, 1)
    p, lse = softmax_with_sink(s, sink_r)
    o = jnp.einsum('nckgqs,ncskd->ncqkgd', p.astype(vw.dtype), vw).reshape(N, nb * BLOCK, Hq, hd)[:, :L]
    lse = lse.transpose(0, 1, 4, 2, 3).reshape(N, nb * BLOCK, Hq)[:, :L]
    return o, lse


def window_attention_step(q, kv_buf, kv_new, step, n_steps, slopes, sink):
    N, T, Hq, hd = q.shape
    L, Hk = kv_buf.shape[1], kv_buf.shape[3]
    G = Hq // Hk
    j = jnp.arange(n_steps + 1)
    idx = L + jnp.arange(T)[:, None] - step * j[None, :]
    valid = idx >= 0
    in_buf = (idx < L)[None, :, :, None, None, None]
    rows = jnp.where(in_buf, kv_buf[:, jnp.clip(idx, 0, L - 1)], kv_new[:, jnp.clip(idx - L, 0, T - 1)])
    kg, vg = rows[:, :, :, 0], rows[:, :, :, 1]
    s = jnp.einsum('ntkgd,ntjkd->ntkgj', q.reshape(N, T, Hk, G, hd), kg,
                   preferred_element_type=jnp.float32) * (hd ** -0.5)
    bias = -(slopes.reshape(Hk, G, 1) * (step * j).astype(jnp.float32))
    s = jnp.where(valid[None, :, None, None, :], s + bias, -jnp.inf)
    sink_r = None if sink is None else sink.astype(jnp.float32).reshape(Hk, G)
    p, lse = softmax_with_sink(s, sink_r)
    o = jnp.einsum('ntkgj,ntjkd->ntkgd', p.astype(vg.dtype), vg).reshape(N, T, Hq, hd)
    return o, lse.reshape(N, T, Hq)


def qkv_dilated(h, w_in, q_gain, k_gain):
    N, T, _ = h.shape
    qkv = jnp.einsum('ntd,dc->ntc', h, w_in).reshape(N, T, N_GROUPS_A, 3, H_A, HEAD_DIM)
    q = rms_norm(qkv[:, :, :, 0], q_gain[:, None, :])
    k = rms_norm(qkv[:, :, :, 1], k_gain[:, None, :])
    return q, k, qkv[:, :, :, 2]


def dilated_group_prompt(q, k, v, dil, n_steps, slopes):
    B, S, H, hd = q.shape
    Ls = S // dil

    def split(a):
        return a.reshape(B, Ls, dil, H, hd).transpose(0, 2, 1, 3, 4).reshape(B * dil, Ls, H, hd)

    o, lse = banded_window_attention(split(q), split(k), split(v), dil, n_steps, slopes, None)
    o = o.reshape(B, dil, Ls, H, hd).transpose(0, 2, 1, 3, 4).reshape(B, S, H, hd)
    lse = lse.reshape(B, dil, Ls, H).transpose(0, 2, 1, 3).reshape(B, S, H)
    return o, lse


def merge_by_denominator(outs, lses):
    w = jax.nn.softmax(jnp.stack(lses, axis=0), axis=0)
    o = jnp.stack(outs, axis=0)
    return jnp.sum(w[..., None].astype(o.dtype) * o, axis=0)


def qkv_swa(h, w_in, q_gain, k_gain):
    N, T, _ = h.shape
    qkv = jnp.einsum('ntd,dc->ntc', h, w_in)
    nq, nk = HQ_B * HEAD_DIM, HKV_B * HEAD_DIM
    q = qkv[..., :nq].reshape(N, T, HQ_B, HEAD_DIM)
    k = qkv[..., nq:nq + nk].reshape(N, T, HKV_B, HEAD_DIM)
    v = qkv[..., nq + nk:].reshape(N, T, HKV_B, HEAD_DIM)
    return rms_norm(q, q_gain), rms_norm(k, k_gain), v


def out_proj(o, w_out):
    N, T = o.shape[:2]
    return jnp.einsum('ntc,cd->ntd', o.reshape(N, T, -1), w_out)


def swiglu(h, w_gu, w_down):
    g, u = jnp.split(h @ w_gu, 2, axis=-1)
    return (jax.nn.silu(g) * u) @ w_down


def moe_swiglu(h, w_router, w_gu, w_down):
    N, T, D = h.shape
    hf = h.reshape(N * T, D)
    logits = jnp.einsum('md,de->me', hf, w_router, preferred_element_type=jnp.float32)
    top_logit, top_idx = lax.top_k(logits, TOP_K)
    top_gate = jax.nn.softmax(top_logit, axis=-1)
    gate = jnp.sum(jax.nn.one_hot(top_idx, N_EXPERTS, dtype=jnp.float32) * top_gate[..., None], axis=1)
    y = jnp.zeros_like(hf)
    for e in range(N_EXPERTS):
        y = y + gate[:, e:e + 1].astype(hf.dtype) * swiglu(hf, w_gu[e], w_down[e])
    return y.reshape(N, T, D)


def setup_inputs(seed: int = 0) -> dict:
    key = jax.random.key(seed)
    ks = iter(jax.random.split(key, 32))

    def nrm(shape, scale=1.0):
        return jax.random.normal(next(ks), shape, jnp.float32) * scale

    ne, no = N_EVEN_LAYERS, N_ODD_LAYERS
    cols_a = N_GROUPS_A * 3 * H_A * HEAD_DIM
    cols_b = (HQ_B + 2 * HKV_B) * HEAD_DIM
    return {
        'x_prompt': nrm((BATCH, SEQ, D_MODEL)),
        'x_sample': nrm((DEC_BATCH, DEC_SEQ, D_MODEL)),
        'cache_a_w128': nrm((ne, DEC_BATCH, min(A_GROUPS[0][0], PAST_LEN), 2, H_A, HEAD_DIM)),
        'cache_a_w512': nrm((ne, DEC_BATCH, min(A_GROUPS[1][0], PAST_LEN), 2, H_A, HEAD_DIM)),
        'cache_a_w2048': nrm((ne, DEC_BATCH, min(A_GROUPS[2][0], PAST_LEN), 2, H_A, HEAD_DIM)),
        'cache_b': nrm((no, DEC_BATCH, min(WINDOW_B, PAST_LEN), 2, HKV_B, HEAD_DIM)),
        'norm_mix_a': 1.0 + nrm((ne, D_MODEL), 0.02),
        'w_in_a': nrm((ne, D_MODEL, cols_a), D_MODEL ** -0.5),
        'q_gain_a': 1.0 + nrm((ne, N_GROUPS_A, HEAD_DIM), 0.02),
        'k_gain_a': 1.0 + nrm((ne, N_GROUPS_A, HEAD_DIM), 0.02),
        'w_out_a': nrm((ne, H_A * HEAD_DIM, D_MODEL), (H_A * HEAD_DIM) ** -0.5),
        'norm_ffn_dense': 1.0 + nrm((ne, D_MODEL), 0.02),
        'w_gu_dense': nrm((ne, D_MODEL, 2 * D_FF_DENSE), D_MODEL ** -0.5),
        'w_down_dense': nrm((ne, D_FF_DENSE, D_MODEL), D_FF_DENSE ** -0.5),
        'norm_mix_b': 1.0 + nrm((no, D_MODEL), 0.02),
        'w_in_b': nrm((no, D_MODEL, cols_b), D_MODEL ** -0.5),
        'q_gain_b': 1.0 + nrm((no, HEAD_DIM), 0.02),
        'k_gain_b': 1.0 + nrm((no, HEAD_DIM), 0.02),
        'sink_b': nrm((no, HQ_B), 0.5),
        'w_out_b': nrm((no, HQ_B * HEAD_DIM, D_MODEL), (HQ_B * HEAD_DIM) ** -0.5),
        'norm_ffn_moe': 1.0 + nrm((no, D_MODEL), 0.02),
        'w_router': nrm((no, D_MODEL, N_EXPERTS), D_MODEL ** -0.5),
        'w_gu_moe': nrm((no, N_EXPERTS, D_MODEL, 2 * D_FF_EXPERT), D_MODEL ** -0.5),
        'w_down_moe': nrm((no, N_EXPERTS, D_FF_EXPERT, D_MODEL), D_FF_EXPERT ** -0.5),
    }


def reference(x_prompt, x_sample, cache_a_w128, cache_a_w512, cache_a_w2048, cache_b,
              norm_mix_a, w_in_a, q_gain_a, k_gain_a, w_out_a, norm_ffn_dense, w_gu_dense, w_down_dense,
              norm_mix_b, w_in_b, q_gain_b, k_gain_b, sink_b, w_out_b, norm_ffn_moe, w_router,
              w_gu_moe, w_down_moe):
    caches_a = (cache_a_w128, cache_a_w512, cache_a_w2048)
    slopes_a = alibi_slopes(H_A)
    slopes_b = alibi_slopes(HQ_B)
    xp, xs = x_prompt, x_sample
    S = xp.shape[1]
    new_a_p = [[] for _ in A_GROUPS]
    new_a_s = [[] for _ in A_GROUPS]
    new_b_p, new_b_s = [], []
    for i in range(DEPTH):
        li = i // 2
        if i % 2 == 0:
            qp, kp, vp = qkv_dilated(rms_norm(xp, norm_mix_a[li]), w_in_a[li], q_gain_a[li], k_gain_a[li])
            qs, ks_, vs = qkv_dilated(rms_norm(xs, norm_mix_a[li]), w_in_a[li], q_gain_a[li], k_gain_a[li])
            outs_p, lses_p, outs_s, lses_s = [], [], [], []
            for g, (window, dil) in enumerate(A_GROUPS):
                n_steps = window // dil
                o, l = dilated_group_prompt(qp[:, :, g], kp[:, :, g], vp[:, :, g], dil, n_steps, slopes_a)
                outs_p.append(o)
                lses_p.append(l)
                kv_new = jnp.stack([ks_[:, :, g], vs[:, :, g]], axis=2)
                o, l = window_attention_step(qs[:, :, g], caches_a[g][li], kv_new, dil, n_steps, slopes_a, None)
                outs_s.append(o)
                lses_s.append(l)
                keep = min(window, S)
                new_a_p[g].append(jnp.stack([kp[:, S - keep:, g], vp[:, S - keep:, g]], axis=2))
                new_a_s[g].append(kv_new)
            xp = xp + out_proj(merge_by_denominator(outs_p, lses_p), w_out_a[li])
            xs = xs + out_proj(merge_by_denominator(outs_s, lses_s), w_out_a[li])
            xp = xp + swiglu(rms_norm(xp, norm_ffn_dense[li]), w_gu_dense[li], w_down_dense[li])
            xs = xs + swiglu(rms_norm(xs, norm_ffn_dense[li]), w_gu_dense[li], w_down_dense[li])
        else:
            qp, kp, vp = qkv_swa(rms_norm(xp, norm_mix_b[li]), w_in_b[li], q_gain_b[li], k_gain_b[li])
            qs, ks_, vs = qkv_swa(rms_norm(xs, norm_mix_b[li]), w_in_b[li], q_gain_b[li], k_gain_b[li])
            op, _ = banded_window_attention(qp, kp, vp, 1, WINDOW_B, slopes_b, sink_b[li])
            kv_new = jnp.stack([ks_, vs], axis=2)
            os_, _ = window_attention_step(qs, cache_b[li], kv_new, 1, WINDOW_B, slopes_b, sink_b[li])
            keep = min(WINDOW_B, S)
            new_b_p.append(jnp.stack([kp[:, S - keep:], vp[:, S - keep:]], axis=2))
            new_b_s.append(kv_new)
            xp = xp + out_proj(op, w_out_b[li])
            xs = xs + out_proj(os_, w_out_b[li])
            xp = xp + moe_swiglu(rms_norm(xp, norm_ffn_moe[li]), w_router[li], w_gu_moe[li], w_down_moe[li])
            xs = xs + moe_swiglu(rms_norm(xs, norm_ffn_moe[li]), w_router[li], w_gu_moe[li], w_down_moe[li])
    a128_prompt = jnp.stack(new_a_p[0], axis=0)
    a128_sample = jnp.stack(new_a_s[0], axis=0)
    a512_prompt = jnp.stack(new_a_p[1], axis=0)
    a512_sample = jnp.stack(new_a_s[1], axis=0)
    a2048_prompt = jnp.stack(new_a_p[2], axis=0)
    a2048_sample = jnp.stack(new_a_s[2], axis=0)
    b_prompt = jnp.stack(new_b_p, axis=0)
    b_sample = jnp.stack(new_b_s, axis=0)
    return (xp, xs, a128_prompt, a128_sample, a512_prompt, a512_sample, a2048_prompt, a2048_sample, b_prompt, b_sample)
```

```python
import functools

import numpy as np
import jax
import jax.numpy as jnp
from jax import lax
from jax.experimental import pallas as pl
from jax.experimental.pallas import tpu as pltpu

F32 = jnp.float32
BF16 = jnp.bfloat16

D_MODEL = 1024
HEAD_DIM = 64
N_HEADS = 16
HKV_B = 2
A_GROUPS = ((128, 1), (512, 4), (2048, 16))
N_STEPS = 128
BLK = 128
D_FF_DENSE = 2816
N_EXPERTS = 8
D_FF_EXPERT = 1408
RMS_EPS = 1e-6
MASKED = -1e30
LANES = 128
MXU_DIM = 256
VMEM_LIMIT = 56 * 1024 * 1024

SLOPES = tuple(float(2.0 ** (-8.0 * (h + 1) / N_HEADS)) for h in range(N_HEADS))


def _cparams(sem):
    return pltpu.CompilerParams(dimension_semantics=sem, vmem_limit_bytes=VMEM_LIMIT)


def _rms(x, g):
    ms = jnp.mean(x * x, axis=-1, keepdims=True)
    return x * lax.rsqrt(ms + RMS_EPS) * g


def _proj_kernel(x_ref, g_ref, w_ref, gain_ref, flag_ref, seg_ref, o_ref, h_scr):
    @pl.when(pl.program_id(1) == 0)
    def _():
        h_scr[...] = _rms(x_ref[...], g_ref[...]).astype(BF16)

    acc = jnp.dot(h_scr[...], w_ref[...], preferred_element_type=F32)
    sq = (acc * acc).astype(BF16)
    tn = acc.shape[1]
    ssq = jnp.concatenate(
        [jnp.dot(sq[:, c * MXU_DIM:(c + 1) * MXU_DIM], seg_ref[...], preferred_element_type=F32)
         for c in range(tn // MXU_DIM)], axis=1)
    nrm = acc * lax.rsqrt(ssq * (1.0 / HEAD_DIM) + RMS_EPS) * gain_ref[...]
    o_ref[...] = jnp.where(flag_ref[...] > 0.0, nrm, acc).astype(o_ref.dtype)


def _proj(x, g, w, gain, flag, tn):
    m, d = x.shape
    n = w.shape[1]
    tm = min(m, 1024)
    seg = jnp.asarray(np.kron(np.eye(MXU_DIM // HEAD_DIM), np.ones((HEAD_DIM, HEAD_DIM))), BF16)
    return pl.pallas_call(
        _proj_kernel,
        grid=(m // tm, n // tn),
        in_specs=[
            pl.BlockSpec((tm, d), lambda i, j: (i, 0)),
            pl.BlockSpec((1, d), lambda i, j: (0, 0)),
            pl.BlockSpec((d, tn), lambda i, j: (0, j)),
            pl.BlockSpec((1, tn), lambda i, j: (0, j)),
            pl.BlockSpec((1, tn), lambda i, j: (0, j)),
            pl.BlockSpec((MXU_DIM, MXU_DIM), lambda i, j: (0, 0)),
        ],
        out_specs=pl.BlockSpec((tm, tn), lambda i, j: (i, j)),
        out_shape=jax.ShapeDtypeStruct((m, n), BF16),
        scratch_shapes=[pltpu.VMEM((tm, d), BF16)],
        compiler_params=_cparams(("parallel", "arbitrary")),
        name="proj_qknorm",
    )(x, g.reshape(1, d), w, gain, flag, seg)


def _band_kernel(sink_ref, q_ref, kp_ref, kc_ref, vp_ref, vc_ref, o_ref, lse_ref, *, dil, qb, kv_heads,
                 has_sink):
    c = pl.program_id(2)
    qi = lax.broadcasted_iota(jnp.int32, (BLK, BLK), 0)
    sj = lax.broadcasted_iota(jnp.int32, (BLK, BLK), 1)
    dist_cur = (qi - sj).astype(F32) * float(dil)
    base_prev_in = jnp.where(sj >= qi, -(dist_cur + float(dil * BLK)), MASKED)
    base_cur = jnp.where(sj <= qi, -dist_cur, MASKED)
    lane = lax.broadcasted_iota(jnp.int32, (BLK, LANES), 1)
    rep = N_HEADS // kv_heads
    for i in range(qb):
        rows = slice(i * BLK, (i + 1) * BLK)
        if i == 0:
            base_prev = jnp.where(c == 0, MASKED, base_prev_in)
        else:
            base_prev = base_prev_in
        lse_acc = jnp.zeros((BLK, LANES), F32)
        for h in range(N_HEADS):
            cols = slice(h * HEAD_DIM, (h + 1) * HEAD_DIM)
            kcols = slice((h // rep) * HEAD_DIM, (h // rep + 1) * HEAD_DIM)
            qh = q_ref[0, rows, cols]
            if i == 0:
                kp, vp = kp_ref[0, :, kcols], vp_ref[0, :, kcols]
            else:
                prev = slice((i - 1) * BLK, i * BLK)
                kp, vp = kc_ref[0, prev, kcols], vc_ref[0, prev, kcols]
            kc, vc = kc_ref[0, rows, kcols], vc_ref[0, rows, kcols]
            nt = (((1,), (1,)), ((), ()))
            s_p = lax.dot_general(qh, kp, nt, preferred_element_type=F32) + SLOPES[h] * base_prev
            s_c = lax.dot_general(qh, kc, nt, preferred_element_type=F32) + SLOPES[h] * base_cur
            m = jnp.maximum(jnp.max(s_p, axis=-1, keepdims=True), jnp.max(s_c, axis=-1, keepdims=True))
            if has_sink:
                m = jnp.maximum(m, sink_ref[h])
            p_p = jnp.exp(s_p - m)
            p_c = jnp.exp(s_c - m)
            den = jnp.sum(p_p, axis=-1, keepdims=True) + jnp.sum(p_c, axis=-1, keepdims=True)
            if has_sink:
                den = den + jnp.exp(sink_ref[h] - m)
            o = (jnp.dot(p_p.astype(BF16), vp, preferred_element_type=F32)
                 + jnp.dot(p_c.astype(BF16), vc, preferred_element_type=F32))
            o_ref[0, rows, cols] = (o * (1.0 / den)).astype(o_ref.dtype)
            lse_acc = jnp.where(lane == h, m + jnp.log(den), lse_acc)
        lse_ref[0, rows, :] = lse_acc


def _band_attention(qkv, sink, *, batch, dil, ncols, q_blk, k_blk, v_blk, kv_heads, has_sink):
    s_len = qkv.shape[0] // batch
    ls = s_len // dil
    qb = min(4, ls // BLK)
    kvw = kv_heads * HEAD_DIM
    view = qkv.reshape(batch, ls, dil * ncols)
    qpr, kpr = ncols // D_MODEL, ncols // kvw
    kern = functools.partial(_band_kernel, dil=dil, qb=qb, kv_heads=kv_heads, has_sink=has_sink)
    o, lse = pl.pallas_call(
        kern,
        grid=(batch, dil, ls // (BLK * qb)),
        in_specs=[
            pl.BlockSpec(memory_space=pltpu.SMEM),
            pl.BlockSpec((1, BLK * qb, D_MODEL), lambda b, r, c: (b, c, r * qpr + q_blk)),
            pl.BlockSpec((1, BLK, kvw), lambda b, r, c: (b, jnp.maximum(c * qb - 1, 0), r * kpr + k_blk)),
            pl.BlockSpec((1, BLK * qb, kvw), lambda b, r, c: (b, c, r * kpr + k_blk)),
            pl.BlockSpec((1, BLK, kvw), lambda b, r, c: (b, jnp.maximum(c * qb - 1, 0), r * kpr + v_blk)),
            pl.BlockSpec((1, BLK * qb, kvw), lambda b, r, c: (b, c, r * kpr + v_blk)),
        ],
        out_specs=[
            pl.BlockSpec((1, BLK * qb, D_MODEL), lambda b, r, c: (b, c, r)),
            pl.BlockSpec((1, BLK * qb, LANES), lambda b, r, c: (b, c, r)),
        ],
        out_shape=[
            jax.ShapeDtypeStruct((batch, ls, dil * D_MODEL), BF16),
            jax.ShapeDtypeStruct((batch, ls, dil * LANES), F32),
        ],
        compiler_params=_cparams(("parallel", "parallel", "arbitrary")),
        name="band_attn_d%d" % dil,
    )(sink, view, view, view, view, view)
    return o.reshape(batch * s_len, D_MODEL), lse.reshape(batch * s_len, LANES)


def _head_expand():
    e = np.zeros((LANES, D_MODEL), np.float32)
    for h in range(N_HEADS):
        e[h, h * HEAD_DIM:(h + 1) * HEAD_DIM] = 1.0
    return jnp.asarray(e, BF16)


def _merge_out_kernel(o0_ref, o1_ref, o2_ref, l0_ref, l1_ref, l2_ref, e_ref, w_ref, x_ref, out_ref):
    l0, l1, l2 = l0_ref[...], l1_ref[...], l2_ref[...]
    top = jnp.maximum(jnp.maximum(l0, l1), l2)
    e0, e1, e2 = jnp.exp(l0 - top), jnp.exp(l1 - top), jnp.exp(l2 - top)
    inv = 1.0 / (e0 + e1 + e2)
    mix = jnp.zeros(out_ref.shape, F32)
    for e, o_ref in ((e0, o0_ref), (e1, o1_ref), (e2, o2_ref)):
        w = jnp.dot((e * inv).astype(BF16), e_ref[...], preferred_element_type=F32)
        mix = mix + w * o_ref[...].astype(F32)
    out_ref[...] = x_ref[...] + jnp.dot(mix.astype(BF16), w_ref[...], preferred_element_type=F32)


def _merge_out(outs, lses, w, x):
    m, d = x.shape
    tm = min(m, 512)
    row = lambda i: (i, 0)
    fixed = lambda i: (0, 0)
    return pl.pallas_call(
        _merge_out_kernel,
        grid=(m // tm,),
        in_specs=[pl.BlockSpec((tm, d), row)] * 3 + [pl.BlockSpec((tm, LANES), row)] * 3 + [
            pl.BlockSpec((LANES, d), fixed), pl.BlockSpec((d, d), fixed), pl.BlockSpec((tm, d), row)],
        out_specs=pl.BlockSpec((tm, d), row),
        out_shape=jax.ShapeDtypeStruct((m, d), F32),
        compiler_params=_cparams(("parallel",)),
        name="merge_outproj",
    )(*outs, *lses, _head_expand(), w, x)


def _out_kernel(o_ref, w_ref, x_ref, out_ref):
    out_ref[...] = x_ref[...] + jnp.dot(o_ref[...].astype(BF16), w_ref[...], preferred_element_type=F32)


def _out_proj(o, w, x):
    m, d = x.shape
    tm = min(m, 1024)
    return pl.pallas_call(
        _out_kernel,
        grid=(m // tm,),
        in_specs=[pl.BlockSpec((tm, d), lambda i: (i, 0)), pl.BlockSpec((d, d), lambda i: (0, 0)),
                  pl.BlockSpec((tm, d), lambda i: (i, 0))],
        out_specs=pl.BlockSpec((tm, d), lambda i: (i, 0)),
        out_shape=jax.ShapeDtypeStruct((m, d), F32),
        compiler_params=_cparams(("parallel",)),
        name="outproj",
    )(o, w, x)


def _silu(g):
    return g * (1.0 / (1.0 + jnp.exp(-g)))


def _ffn_kernel(x_ref, g_ref, wgu_ref, wd_ref, out_ref, *, d_ff, chunk):
    x = x_ref[...]
    h = _rms(x, g_ref[...]).astype(BF16)
    acc = jnp.zeros(x.shape, F32)
    for c in range(d_ff // chunk):
        gate = jnp.dot(h, wgu_ref[:, c * chunk:(c + 1) * chunk], preferred_element_type=F32)
        up = jnp.dot(h, wgu_ref[:, d_ff + c * chunk:d_ff + (c + 1) * chunk], preferred_element_type=F32)
        act = (_silu(gate) * up).astype(BF16)
        acc = acc + jnp.dot(act, wd_ref[c * chunk:(c + 1) * chunk, :], preferred_element_type=F32)
    out_ref[...] = x + acc


def _ffn_dense(x, g, wgu, wd):
    m, d = x.shape
    tm = min(m, 512)
    d_ff = wd.shape[0]
    kern = functools.partial(_ffn_kernel, d_ff=d_ff, chunk=d_ff // 2)
    return pl.pallas_call(
        kern,
        grid=(m // tm,),
        in_specs=[pl.BlockSpec((tm, d), lambda i: (i, 0)), pl.BlockSpec((1, d), lambda i: (0, 0)),
                  pl.BlockSpec((d, 2 * d_ff), lambda i: (0, 0), pipeline_mode=pl.Buffered(1)),
                  pl.BlockSpec((d_ff, d), lambda i: (0, 0), pipeline_mode=pl.Buffered(1))],
        out_specs=pl.BlockSpec((tm, d), lambda i: (i, 0)),
        out_shape=jax.ShapeDtypeStruct((m, d), F32),
        compiler_params=_cparams(("parallel",)),
        name="ffn_dense",
    )(x, g.reshape(1, d), wgu, wd)


def _router_kernel(x_ref, g_ref, wr_ref, idx_ref, gate_ref):
    h = _rms(x_ref[...], g_ref[...])
    logits = jnp.dot(h, wr_ref[...], preferred_element_type=F32, precision=lax.Precision.HIGHEST)
    lane = lax.broadcasted_iota(jnp.int32, logits.shape, 1)
    logits = jnp.where(lane < N_EXPERTS, logits, -jnp.inf)
    t1 = jnp.max(logits, axis=-1, keepdims=True)
    i1 = jnp.min(jnp.where(logits == t1, lane, LANES), axis=-1, keepdims=True)
    rest = jnp.where(lane == i1, -jnp.inf, logits)
    t2 = jnp.max(rest, axis=-1, keepdims=True)
    i2 = jnp.min(jnp.where(rest == t2, lane, LANES), axis=-1, keepdims=True)
    e2 = jnp.exp(t2 - t1)
    g1 = 1.0 / (1.0 + e2)
    g2 = e2 / (1.0 + e2)
    idx_ref[...] = jnp.where(lane == 0, i1, jnp.where(lane == 1, i2, 0))
    gate_ref[...] = jnp.where(lane == 0, g1, jnp.where(lane == 1, g2, 0.0))


def _router(x, g, w_router):
    m, d = x.shape
    tm = min(m, 1024)
    wr = jnp.zeros((d, LANES), F32).at[:, :N_EXPERTS].set(w_router)
    idx, gate = pl.pallas_call(
        _router_kernel,
        grid=(m // tm,),
        in_specs=[pl.BlockSpec((tm, d), lambda i: (i, 0)), pl.BlockSpec((1, d), lambda i: (0, 0)),
                  pl.BlockSpec((d, LANES), lambda i: (0, 0))],
        out_specs=[pl.BlockSpec((tm, LANES), lambda i: (i, 0))] * 2,
        out_shape=[jax.ShapeDtypeStruct((m, LANES), jnp.int32), jax.ShapeDtypeStruct((m, LANES), F32)],
        compiler_params=_cparams(("parallel",)),
        name="router_top2",
    )(x, g.reshape(1, d), wr)
    return idx[:, :2], gate[:, :2]


def _row_copy(src_hbm, dst, sem, src_row, dst_row):
    return pltpu.make_async_copy(src_hbm.at[pl.ds(src_row, 1)], dst.at[pl.ds(dst_row, 1)], sem)


def _gather_kernel(idx_ref, x_hbm, o_ref, sem, *, rows):
    def start(r, carry):
        _row_copy(x_hbm, o_ref, sem, idx_ref[0, 0, r], r).start()
        return carry

    def wait(r, carry):
        _row_copy(x_hbm, o_ref, sem, 0, r).wait()
        return carry

    lax.fori_loop(0, rows, start, 0)
    lax.fori_loop(0, rows, wait, 0)


def _gather_rows(x, idx, rows):
    n = idx.shape[0]
    d = x.shape[1]
    return pl.pallas_call(
        functools.partial(_gather_kernel, rows=rows),
        grid=(n // rows,),
        in_specs=[pl.BlockSpec((1, 1, rows), lambda i: (i, 0, 0), memory_space=pltpu.SMEM),
                  pl.BlockSpec(memory_space=pl.ANY)],
        out_specs=pl.BlockSpec((rows, d), lambda i: (i, 0)),
        scratch_shapes=[pltpu.SemaphoreType.DMA(())],
        out_shape=jax.ShapeDtypeStruct((n, d), x.dtype),
        compiler_params=_cparams(("arbitrary",)),
        name="moe_gather",
    )(idx.reshape(n // rows, 1, rows), x)


def _expert_kernel(te_ref, nv_ref, xs_ref, g_ref, gate_ref, wgu_ref, wd_ref, o_ref, *, d_ff):
    @pl.when(pl.program_id(0) < nv_ref[0])
    def _():
        h = _rms(xs_ref[...], g_ref[...]).astype(BF16)
        gate = jnp.dot(h, wgu_ref[0, :, :d_ff], preferred_element_type=F32)
        up = jnp.dot(h, wgu_ref[0, :, d_ff:], preferred_element_type=F32)
        act = (_silu(gate) * up).astype(BF16)
        y = jnp.dot(act, wd_ref[0], preferred_element_type=F32)
        o_ref[...] = y * gate_ref[:, 0:1]

    @pl.when(pl.program_id(0) >= nv_ref[0])
    def _():
        o_ref[...] = jnp.zeros(o_ref.shape, o_ref.dtype)


def _experts(xs, g, gate_rows, tile_expert, n_valid, wgu, wd, tm):
    n, d = xs.shape
    d_ff = wd.shape[1]
    return pl.pallas_call(
        functools.partial(_expert_kernel, d_ff=d_ff),
        grid_spec=pltpu.PrefetchScalarGridSpec(
            num_scalar_prefetch=2,
            grid=(n // tm,),
            in_specs=[
                pl.BlockSpec((tm, d), lambda i, te, nv: (i, 0)),
                pl.BlockSpec((1, d), lambda i, te, nv: (0, 0)),
                pl.BlockSpec((tm, LANES), lambda i, te, nv: (i, 0)),
                pl.BlockSpec((1, d, 2 * d_ff), lambda i, te, nv: (te[i], 0, 0)),
                pl.BlockSpec((1, d_ff, d), lambda i, te, nv: (te[i], 0, 0)),
            ],
            out_specs=pl.BlockSpec((tm, d), lambda i, te, nv: (i, 0)),
        ),
        out_shape=jax.ShapeDtypeStruct((n, d), F32),
        compiler_params=_cparams(("arbitrary",)),
        name="moe_experts",
    )(tile_expert, n_valid, xs, g.reshape(1, d), gate_rows, wgu, wd)


def _combine_kernel(pos_ref, x_ref, y_hbm, o_ref, buf0, buf1, sem, *, rows):
    def start(r, carry):
        _row_copy(y_hbm, buf0, sem.at[0], pos_ref[0, 0, 2 * r], r).start()
        _row_copy(y_hbm, buf1, sem.at[1], pos_ref[0, 0, 2 * r + 1], r).start()
        return carry

    def wait(r, carry):
        _row_copy(y_hbm, buf0, sem.at[0], 0, r).wait()
        _row_copy(y_hbm, buf1, sem.at[1], 0, r).wait()
        return carry

    lax.fori_loop(0, rows, start, 0)
    lax.fori_loop(0, rows, wait, 0)
    o_ref[...] = x_ref[...] + (buf0[...] + buf1[...])


def _combine(x, ys, pos, rows):
    m, d = x.shape
    return pl.pallas_call(
        functools.partial(_combine_kernel, rows=rows),
        grid=(m // rows,),
        in_specs=[pl.BlockSpec((1, 1, 2 * rows), lambda i: (i, 0, 0), memory_space=pltpu.SMEM),
                  pl.BlockSpec((rows, d), lambda i: (i, 0)), pl.BlockSpec(memory_space=pl.ANY)],
        out_specs=pl.BlockSpec((rows, d), lambda i: (i, 0)),
        scratch_shapes=[pltpu.VMEM((rows, d), F32), pltpu.VMEM((rows, d), F32), pltpu.SemaphoreType.DMA((2,))],
        out_shape=jax.ShapeDtypeStruct((m, d), F32),
        compiler_params=_cparams(("arbitrary",)),
        name="moe_combine",
    )(pos.reshape(m // rows, 1, 2 * rows), x, ys)


def _moe(x, g, w_router, wgu, wd, tm):
    m, d = x.shape
    top_idx, top_gate = _router(x, g, w_router)
    e_flat = top_idx.reshape(-1)
    onehot = (e_flat[:, None] == jnp.arange(N_EXPERTS)[None, :]).astype(jnp.int32)
    counts = jnp.sum(onehot, axis=0)
    rank = jnp.sum((jnp.cumsum(onehot, axis=0) - onehot) * onehot, axis=1)
    padded = ((counts + tm - 1) // tm) * tm
    ends = jnp.cumsum(padded)
    pos = (ends - padded)[e_flat] + rank
    n_tiles = -(-(2 * m + N_EXPERTS * (tm - 1)) // tm)
    n_rows = n_tiles * tm
    src_token = jnp.zeros((n_rows,), jnp.int32).at[pos].set(jnp.arange(2 * m, dtype=jnp.int32) // 2)
    gate_rows = jnp.zeros((n_rows,), F32).at[pos].set(top_gate.reshape(-1))
    gate_rows = jnp.broadcast_to(gate_rows[:, None], (n_rows, LANES))
    tile_start = jnp.arange(n_tiles, dtype=jnp.int32) * tm
    tile_expert = jnp.minimum(jnp.sum((tile_start[:, None] >= ends[None, :]).astype(jnp.int32), axis=1),
                              N_EXPERTS - 1).astype(jnp.int32)
    n_valid = (ends[-1] // tm).astype(jnp.int32).reshape(1)

    rows = min(m, 256)
    xs = _gather_rows(x, src_token, tm)
    ys = _experts(xs, g, gate_rows, tile_expert, n_valid, wgu, wd, tm)
    return _combine(x, ys, pos.astype(jnp.int32), rows)


def _seg_sum():
    return _head_expand().T


def _split_dot(a, b):
    hi = a.astype(BF16)
    lo = (a - hi.astype(F32)).astype(BF16)
    return (jnp.dot(hi, b, preferred_element_type=F32) + jnp.dot(lo, b, preferred_element_type=F32))


def _step_scores(q, k_rows, k_new, seg, bias_rows, slope_row):
    s_rows = _split_dot(k_rows * q, seg) + bias_rows * slope_row
    s_new = _split_dot(k_new * q, seg)
    return s_rows, s_new


def _step_group(q, k_rows, v_rows, k_new, v_new, seg, expand, bias_rows, slope_row, sink_row):
    s_rows, s_new = _step_scores(q, k_rows, k_new, seg, bias_rows, slope_row)
    m = jnp.maximum(jnp.max(s_rows, axis=0, keepdims=True), s_new)
    if sink_row is not None:
        m = jnp.maximum(m, sink_row)
    p_rows = jnp.exp(s_rows - m)
    p_new = jnp.exp(s_new - m)
    den = jnp.sum(p_rows, axis=0, keepdims=True) + p_new
    if sink_row is not None:
        den = den + jnp.exp(sink_row - m)
    pv = jnp.sum(_split_dot(p_rows, expand) * v_rows, axis=0, keepdims=True) + _split_dot(p_new, expand) * v_new
    return pv, den, m + jnp.log(den)


def _step_a_kernel(qkv_ref, c0_ref, c1_ref, c2_ref, seg_ref, exp_ref, slope_ref, o_ref):
    seg, expand = seg_ref[...], exp_ref[...]
    slope_row = slope_ref[...]
    steps = (N_STEPS - lax.broadcasted_iota(jnp.int32, (N_STEPS, LANES), 0)).astype(F32)
    outs, lses = [], []
    for g, (c_ref, (_, dil)) in enumerate(zip((c0_ref, c1_ref, c2_ref), A_GROUPS)):
        off = g * 3 * D_MODEL
        q = qkv_ref[0, :, off:off + D_MODEL].astype(F32)
        k_new = qkv_ref[0, :, off + D_MODEL:off + 2 * D_MODEL].astype(F32)
        v_new = qkv_ref[0, :, off + 2 * D_MODEL:off + 3 * D_MODEL].astype(F32)
        k_rows = c_ref[0, :, :D_MODEL]
        v_rows = c_ref[0, :, D_MODEL:]
        pv, den, lse = _step_group(q, k_rows, v_rows, k_new, v_new, seg, expand,
                                   -steps * float(dil), slope_row, None)
        outs.append(pv * (1.0 / _split_dot(den, expand)))
        lses.append(lse)
    top = jnp.maximum(jnp.maximum(lses[0], lses[1]), lses[2])
    es = [jnp.exp(l - top) for l in lses]
    inv = 1.0 / (es[0] + es[1] + es[2])
    mix = sum(_split_dot(e * inv, expand) * o for e, o in zip(es, outs))
    o_ref[0] = mix


def _step_attention_a(qkv, caches):
    n = qkv.shape[0]
    views = [c.reshape(n, N_STEPS, dil * 2 * D_MODEL) for c, (_, dil) in zip(caches, A_GROUPS)]
    slope_row = jnp.zeros((1, LANES), F32).at[0, :N_HEADS].set(jnp.asarray(SLOPES, F32))
    fixed = lambda i: (0, 0)
    out = pl.pallas_call(
        _step_a_kernel,
        grid=(n,),
        in_specs=[pl.BlockSpec((1, 1, qkv.shape[1]), lambda i: (i, 0, 0))]
        + [pl.BlockSpec((1, N_STEPS, 2 * D_MODEL), lambda i: (i, 0, 0))] * 3
        + [pl.BlockSpec((D_MODEL, LANES), fixed), pl.BlockSpec((LANES, D_MODEL), fixed),
           pl.BlockSpec((1, LANES), fixed)],
        out_specs=pl.BlockSpec((1, 1, D_MODEL), lambda i: (i, 0, 0)),
        out_shape=jax.ShapeDtypeStruct((n, 1, D_MODEL), F32),
        compiler_params=_cparams(("parallel",)),
        name="step_attn_a",
    )(qkv.reshape(n, 1, -1), *views, _seg_sum(), _head_expand(), slope_row)
    return out.reshape(n, D_MODEL)


def _step_b_kernel(qkv_ref, c_ref, seg_ref, exp_ref, slope_ref, sink_ref, o_ref):
    seg, expand = seg_ref[...], exp_ref[...]
    steps = (N_STEPS - lax.broadcasted_iota(jnp.int32, (N_STEPS, LANES), 0)).astype(F32)
    rep = N_HEADS // HKV_B
    kvw = HKV_B * HEAD_DIM

    def widen(a):
        return jnp.concatenate([a[:, (h // rep) * HEAD_DIM:(h // rep + 1) * HEAD_DIM] for h in range(N_HEADS)],
                               axis=1)

    q = qkv_ref[0, :, :D_MODEL].astype(F32)
    k_new = widen(qkv_ref[0, :, D_MODEL:D_MODEL + kvw].astype(F32))
    v_new = widen(qkv_ref[0, :, D_MODEL + kvw:].astype(F32))
    k_rows = widen(c_ref[0, :, :kvw])
    v_rows = widen(c_ref[0, :, kvw:])
    pv, den, _ = _step_group(q, k_rows, v_rows, k_new, v_new, seg, expand, -steps, slope_ref[...],
                             sink_ref[...])
    o_ref[0] = pv * (1.0 / _split_dot(den, expand))


def _step_attention_b(qkv, cache, sink):
    n = qkv.shape[0]
    kvw = HKV_B * HEAD_DIM
    slope_row = jnp.zeros((1, LANES), F32).at[0, :N_HEADS].set(jnp.asarray(SLOPES, F32))
    sink_row = jnp.full((1, LANES), MASKED, F32).at[0, :N_HEADS].set(sink.astype(F32))
    fixed = lambda i: (0, 0)
    out = pl.pallas_call(
        _step_b_kernel,
        grid=(n,),
        in_specs=[pl.BlockSpec((1, 1, qkv.shape[1]), lambda i: (i, 0, 0)),
                  pl.BlockSpec((1, N_STEPS, 2 * kvw), lambda i: (i, 0, 0)),
                  pl.BlockSpec((D_MODEL, LANES), fixed), pl.BlockSpec((LANES, D_MODEL), fixed),
                  pl.BlockSpec((1, LANES), fixed), pl.BlockSpec((1, LANES), fixed)],
        out_specs=pl.BlockSpec((1, 1, D_MODEL), lambda i: (i, 0, 0)),
        out_shape=jax.ShapeDtypeStruct((n, 1, D_MODEL), F32),
        compiler_params=_cparams(("parallel",)),
        name="step_attn_b",
    )(qkv.reshape(n, 1, -1), cache.reshape(n, N_STEPS, 2 * kvw), _seg_sum(), _head_expand(), slope_row,
      sink_row)
    return out.reshape(n, D_MODEL)


def _tile_heads(v):
    return jnp.tile(v.astype(F32), N_HEADS)


def _kv_rows(qkv, batch, keep, k_col, width, heads):
    s_len = qkv.shape[0] // batch
    kv = qkv.reshape(batch, s_len, -1)[:, s_len - keep:, k_col:k_col + 2 * width]
    return kv.reshape(batch, keep, 2, heads, HEAD_DIM).astype(F32)


def kernel(x_prompt, x_sample, cache_a_w128, cache_a_w512, cache_a_w2048, cache_b, norm_mix_a, w_in_a, q_gain_a, k_gain_a, w_out_a, norm_ffn_dense, w_gu_dense, w_down_dense, norm_mix_b, w_in_b, q_gain_b, k_gain_b, sink_b, w_out_b, norm_ffn_moe, w_router, w_gu_moe, w_down_moe):
    batch, s_len, d = x_prompt.shape
    n_dec = x_sample.shape[0]
    assert x_sample.shape[1] == 1 and d == D_MODEL
    caches_a = (cache_a_w128, cache_a_w512, cache_a_w2048)
    xp = x_prompt.reshape(batch * s_len, d)
    xs = x_sample.reshape(n_dec, d)
    q_scale = HEAD_DIM ** -0.5
    outs = []

    li = 0
    w_in = w_in_a[li].astype(BF16)
    cols_a = w_in.shape[1]
    gain = jnp.concatenate([jnp.concatenate([_tile_heads(q_gain_a[li, g]) * q_scale, _tile_heads(k_gain_a[li, g]),
                                             jnp.ones((d,), F32)]) for g in range(len(A_GROUPS))]).reshape(1, cols_a)
    flag = jnp.tile(jnp.concatenate([jnp.ones((2 * d,), F32), jnp.zeros((d,), F32)]), len(A_GROUPS)).reshape(1, cols_a)
    qkv_p = _proj(xp, norm_mix_a[li], w_in, gain, flag, d)
    qkv_s = _proj(xs, norm_mix_a[li], w_in, gain, flag, d)
    no_sink = jnp.zeros((N_HEADS,), F32)
    o_p, l_p = [], []
    for g, (window, dil) in enumerate(A_GROUPS):
        o, l = _band_attention(qkv_p, no_sink, batch=batch, dil=dil, ncols=cols_a, q_blk=3 * g, k_blk=3 * g + 1,
                               v_blk=3 * g + 2, kv_heads=N_HEADS, has_sink=False)
        o_p.append(o)
        l_p.append(l)
        keep = min(window, s_len)
        outs.append(_kv_rows(qkv_p, batch, keep, (3 * g + 1) * d, d, N_HEADS)[None])
        outs.append(_kv_rows(qkv_s, n_dec, 1, (3 * g + 1) * d, d, N_HEADS)[None])
    w_out = w_out_a[li].astype(BF16)
    xp = _merge_out(o_p, l_p, w_out, xp)
    xs = _out_proj(_step_attention_a(qkv_s, [c[li] for c in caches_a]), w_out, xs)
    wgu, wd = w_gu_dense[li].astype(BF16), w_down_dense[li].astype(BF16)
    xp = _ffn_dense(xp, norm_ffn_dense[li], wgu, wd)
    xs = _ffn_dense(xs, norm_ffn_dense[li], wgu, wd)

    w_in = w_in_b[li].astype(BF16)
    cols_b = w_in.shape[1]
    kvw = HKV_B * HEAD_DIM
    gain = jnp.concatenate([_tile_heads(q_gain_b[li]) * q_scale, jnp.tile(k_gain_b[li].astype(F32), HKV_B),
                            jnp.ones((kvw,), F32)]).reshape(1, cols_b)
    flag = jnp.concatenate([jnp.ones((d + kvw,), F32), jnp.zeros((kvw,), F32)]).reshape(1, cols_b)
    qkv_p = _proj(xp, norm_mix_b[li], w_in, gain, flag, cols_b)
    qkv_s = _proj(xs, norm_mix_b[li], w_in, gain, flag, cols_b)
    sink = sink_b[li].astype(F32)
    o, _ = _band_attention(qkv_p, sink, batch=batch, dil=1, ncols=cols_b, q_blk=0, k_blk=d // kvw,
                           v_blk=d // kvw + 1, kv_heads=HKV_B, has_sink=True)
    outs.append(_kv_rows(qkv_p, batch, min(N_STEPS, s_len), d, kvw, HKV_B)[None])
    outs.append(_kv_rows(qkv_s, n_dec, 1, d, kvw, HKV_B)[None])
    w_out = w_out_b[li].astype(BF16)
    xp = _out_proj(o, w_out, xp)
    xs = _out_proj(_step_attention_b(qkv_s, cache_b[li], sink), w_out, xs)
    wgu, wd = w_gu_moe[li].astype(BF16), w_down_moe[li].astype(BF16)
    xp = _moe(xp, norm_ffn_moe[li], w_router[li], wgu, wd, 512)
    xs = _moe(xs, norm_ffn_moe[li], w_router[li], wgu, wd, 128)

    return (xp.reshape(batch, s_len, d), xs.reshape(n_dec, 1, d), *outs)
```

```python
import functools

import numpy as np
import jax
import jax.numpy as jnp
from jax import lax
from jax.experimental import pallas as pl
from jax.experimental.pallas import tpu as pltpu

F32 = jnp.float32
BF16 = jnp.bfloat16

D_MODEL = 1024
HEAD_DIM = 64
N_HEADS = 16
HKV_B = 2
A_GROUPS = ((128, 1), (512, 4), (2048, 16))
N_STEPS = 128
BLK = 128
N_EXPERTS = 8
RMS_EPS = 1e-6
MASKED = -1e30
LANES = 128
SUBLANES = 8
MXU_DIM = 256
VMEM_LIMIT = 56 * 1024 * 1024
D_TILES = D_MODEL // LANES

SLOPES = tuple(float(2.0 ** (-8.0 * (h + 1) / N_HEADS)) for h in range(N_HEADS))


def _cparams(sem):
    return pltpu.CompilerParams(dimension_semantics=sem, vmem_limit_bytes=VMEM_LIMIT)


def _rms(x, g):
    ms = jnp.mean(x * x, axis=-1, keepdims=True)
    return x * lax.rsqrt(ms + RMS_EPS) * g


def _from_tiles(ref):
    return jnp.concatenate([ref[:, c, :] for c in range(D_TILES)], axis=1)


def _to_tiles(ref, val):
    for c in range(D_TILES):
        ref[:, c, :] = val[:, c * LANES:(c + 1) * LANES]


def _proj_kernel(x_ref, g_ref, w_ref, gain_ref, flag_ref, seg_ref, o_ref, h_scr, *res_scr, dil):
    @pl.when(pl.program_id(1) == 0)
    def _():
        h_scr[...] = _rms(x_ref[...], g_ref[...]).astype(BF16)

    acc = jnp.dot(h_scr[...], w_ref[...], preferred_element_type=F32)
    sq = (acc * acc).astype(BF16)
    tm, tn = acc.shape
    ssq = jnp.concatenate(
        [jnp.dot(sq[:, c * MXU_DIM:(c + 1) * MXU_DIM], seg_ref[...], preferred_element_type=F32)
         for c in range(tn // MXU_DIM)], axis=1)
    nrm = acc * lax.rsqrt(ssq * (1.0 / HEAD_DIM) + RMS_EPS) * gain_ref[...]
    res = jnp.where(flag_ref[...] > 0.0, nrm, acc)
    if dil == 1:
        o_ref[0, 0] = res.astype(o_ref.dtype)
    else:
        scr = res_scr[0]
        for c in range(tn // LANES):
            cols = slice(c * LANES, (c + 1) * LANES)
            scr[c] = res[:, cols]
            for r in range(dil):
                o_ref[0, r, :, cols] = scr[c, pl.ds(r, tm // dil, stride=dil), :].astype(o_ref.dtype)


def _proj(x, g, w, gain, flag, *, batch, dil, tn, col0, ncols):
    m, d = x.shape
    s_len = m // batch
    tm = min(s_len, 1024)
    tpb = s_len // tm
    seg = jnp.asarray(np.kron(np.eye(MXU_DIM // HEAD_DIM), np.ones((HEAD_DIM, HEAD_DIM))), BF16)
    scratch = [pltpu.VMEM((tm, d), BF16)] + ([pltpu.VMEM((tn // LANES, tm, LANES), F32)] if dil > 1 else [])
    return pl.pallas_call(
        functools.partial(_proj_kernel, dil=dil),
        grid=(m // tm, ncols // tn),
        in_specs=[
            pl.BlockSpec((tm, d), lambda i, j: (i, 0)),
            pl.BlockSpec((1, d), lambda i, j: (0, 0)),
            pl.BlockSpec((d, tn), lambda i, j: (0, col0 + j)),
            pl.BlockSpec((1, tn), lambda i, j: (0, col0 + j)),
            pl.BlockSpec((1, tn), lambda i, j: (0, col0 + j)),
            pl.BlockSpec((MXU_DIM, MXU_DIM), lambda i, j: (0, 0)),
        ],
        out_specs=pl.BlockSpec((1, dil, tm // dil, tn), lambda i, j: (i // tpb, 0, i % tpb, j)),
        out_shape=jax.ShapeDtypeStruct((batch, dil, s_len // dil, ncols), BF16),
        scratch_shapes=scratch,
        compiler_params=_cparams(("parallel", "arbitrary")),
        name="proj_qknorm_d%d" % dil,
    )(x, g.reshape(1, d), w, gain, flag, seg)


def _band_kernel(sink_ref, q_ref, kp_ref, kc_ref, vp_ref, vc_ref, o_ref, lse_ref, k_all, v_all, *, dil, qb,
                 kv_heads, has_sink):
    c = pl.program_id(2)
    k_all[0:BLK] = kp_ref[0, 0]
    k_all[BLK:] = kc_ref[0, 0]
    v_all[0:BLK] = vp_ref[0, 0]
    v_all[BLK:] = vc_ref[0, 0]

    qi = lax.broadcasted_iota(jnp.int32, (BLK, 2 * BLK), 0)
    sj = lax.broadcasted_iota(jnp.int32, (BLK, 2 * BLK), 1)
    dist = qi - sj + BLK
    base = jnp.where((dist >= 0) & (dist <= N_STEPS), -(dist.astype(F32) * float(dil)), MASKED)
    base_first = jnp.where(sj >= BLK, base, MASKED)
    low_k = lax.broadcasted_iota(jnp.int32, (2 * BLK, LANES), 1) < HEAD_DIM
    low_q = lax.broadcasted_iota(jnp.int32, (BLK, LANES), 1) < HEAD_DIM
    zeros_k = jnp.zeros((2 * BLK, LANES), BF16)
    ones_a = jnp.where(low_k, 1.0, 0.0).astype(BF16)
    ones_b = jnp.where(low_k, 0.0, 1.0).astype(BF16)
    nt = (((1,), (1,)), ((), ()))

    for i in range(qb):
        rows = slice(i * BLK, (i + 1) * BLK)
        win = slice(i * BLK, (i + 2) * BLK)
        base_i = jnp.where(c == 0, base_first, base) if i == 0 else base
        if kv_heads != N_HEADS:
            kw, vw = k_all[win, :], v_all[win, :]
            zero_half = jnp.zeros((2 * BLK, HEAD_DIM), BF16)
            shared = {}
            for kv in range(kv_heads):
                k_kv = kw[:, kv * HEAD_DIM:(kv + 1) * HEAD_DIM]
                v_kv = vw[:, kv * HEAD_DIM:(kv + 1) * HEAD_DIM]
                shared[kv] = tuple((jnp.concatenate([t, zero_half], axis=1), jnp.concatenate([zero_half, t], axis=1))
                                   for t in (k_kv, v_kv))
        for pair in range(N_HEADS // 2):
            pc = slice(pair * LANES, (pair + 1) * LANES)
            ha, hb = 2 * pair, 2 * pair + 1
            qp = q_ref[0, 0, rows, pc]
            if kv_heads == N_HEADS:
                kw, vw = k_all[win, pc], v_all[win, pc]
                k_a, k_b = jnp.where(low_k, kw, zeros_k), jnp.where(low_k, zeros_k, kw)
                v_a, v_b = jnp.where(low_k, vw, zeros_k), jnp.where(low_k, zeros_k, vw)
            else:
                (k_a, k_b), (v_a, v_b) = shared[ha // (N_HEADS // kv_heads)]
            s_a = lax.dot_general(qp, k_a, nt, preferred_element_type=F32) + SLOPES[ha] * base_i
            s_b = lax.dot_general(qp, k_b, nt, preferred_element_type=F32) + SLOPES[hb] * base_i
            m_a = jnp.max(s_a, axis=-1, keepdims=True)
            m_b = jnp.max(s_b, axis=-1, keepdims=True)
            if has_sink:
                m_a = jnp.maximum(m_a, sink_ref[ha])
                m_b = jnp.maximum(m_b, sink_ref[hb])
            p_a = jnp.exp(s_a - m_a).astype(BF16)
            p_b = jnp.exp(s_b - m_b).astype(BF16)
            res = (jnp.dot(p_a, jnp.concatenate([v_a, ones_a], axis=1), preferred_element_type=F32)
                   + jnp.dot(p_b, jnp.concatenate([v_b, ones_b], axis=1), preferred_element_type=F32))
            den = res[:, LANES:]
            m_pair = jnp.where(low_q, m_a, m_b)
            if has_sink:
                den = den + jnp.where(low_q, jnp.exp(sink_ref[ha] - m_a), jnp.exp(sink_ref[hb] - m_b))
            o_ref[0, 0, rows, pc] = (res[:, :LANES] * (1.0 / den)).astype(o_ref.dtype)
            lse_ref[0, 0, rows, pc] = m_pair + jnp.log(den)


def _band_attention(qkv, sink, *, q_blk, k_blk, v_blk, kv_heads, has_sink):
    batch, dil, ls, _ = qkv.shape
    qb = min(4, ls // BLK)
    rows = BLK * qb
    kvw = kv_heads * HEAD_DIM
    kern = functools.partial(_band_kernel, dil=dil, qb=qb, kv_heads=kv_heads, has_sink=has_sink)
    prev = lambda b, r, c: jnp.maximum(c * qb - 1, 0)
    return pl.pallas_call(
        kern,
        grid=(batch, dil, ls // rows),
        in_specs=[
            pl.BlockSpec(memory_space=pltpu.SMEM),
            pl.BlockSpec((1, 1, rows, D_MODEL), lambda b, r, c: (b, r, c, q_blk)),
            pl.BlockSpec((1, 1, BLK, kvw), lambda b, r, c: (b, r, prev(b, r, c), k_blk)),
            pl.BlockSpec((1, 1, rows, kvw), lambda b, r, c: (b, r, c, k_blk)),
            pl.BlockSpec((1, 1, BLK, kvw), lambda b, r, c: (b, r, prev(b, r, c), v_blk)),
            pl.BlockSpec((1, 1, rows, kvw), lambda b, r, c: (b, r, c, v_blk)),
        ],
        out_specs=[pl.BlockSpec((1, 1, rows, D_MODEL), lambda b, r, c: (b, r, c, 0))] * 2,
        out_shape=[jax.ShapeDtypeStruct((batch, dil, ls, D_MODEL), BF16),
                   jax.ShapeDtypeStruct((batch, dil, ls, D_MODEL), F32)],
        scratch_shapes=[pltpu.VMEM((BLK + rows, kvw), BF16), pltpu.VMEM((BLK + rows, kvw), BF16)],
        compiler_params=_cparams(("parallel", "parallel", "arbitrary")),
        name="band_attn_d%d" % dil,
    )(sink, qkv, qkv, qkv, qkv, qkv)


def _merge_out_kernel(*refs, dils):
    o_refs, l_refs = refs[0:3], refs[3:6]
    w_ref, x_ref, out_ref = refs[6:9]
    scr = refs[9:]
    tm = x_ref.shape[0]
    outs, lses = [], []
    k = 0
    for o_ref, l_ref, dil in zip(o_refs, l_refs, dils):
        if dil == 1:
            outs.append(o_ref[0, 0].astype(F32))
            lses.append(l_ref[0, 0])
        else:
            so, sl = scr[k], scr[k + 1]
            k += 2
            for c in range(D_TILES):
                cols = slice(c * LANES, (c + 1) * LANES)
                for r in range(dil):
                    so[c, pl.ds(r, tm // dil, stride=dil), :] = o_ref[0, r, :, cols].astype(F32)
                    sl[c, pl.ds(r, tm // dil, stride=dil), :] = l_ref[0, r, :, cols]
            outs.append(jnp.concatenate([so[c] for c in range(D_TILES)], axis=1))
            lses.append(jnp.concatenate([sl[c] for c in range(D_TILES)], axis=1))
    top = jnp.maximum(jnp.maximum(lses[0], lses[1]), lses[2])
    es = [jnp.exp(l - top) for l in lses]
    mix = (es[0] * outs[0] + es[1] * outs[1] + es[2] * outs[2]) * (1.0 / (es[0] + es[1] + es[2]))
    out_ref[...] = x_ref[...] + jnp.dot(mix.astype(BF16), w_ref[...], preferred_element_type=F32)


def _merge_out(outs, lses, w, x):
    m, d = x.shape
    batch = outs[0].shape[0]
    s_len = m // batch
    tm = min(s_len, 256)
    tpb = s_len // tm
    dils = tuple(o.shape[1] for o in outs)
    grp = [pl.BlockSpec((1, dil, tm // dil, d), lambda i: (i // tpb, 0, i % tpb, 0)) for dil in dils]
    scratch = [pltpu.VMEM((D_TILES, tm, LANES), F32) for dil in dils if dil > 1 for _ in range(2)]
    return pl.pallas_call(
        functools.partial(_merge_out_kernel, dils=dils),
        grid=(m // tm,),
        in_specs=grp + grp + [pl.BlockSpec((d, d), lambda i: (0, 0)), pl.BlockSpec((tm, d), lambda i: (i, 0))],
        out_specs=pl.BlockSpec((tm, d), lambda i: (i, 0)),
        out_shape=jax.ShapeDtypeStruct((m, d), F32),
        scratch_shapes=scratch,
        compiler_params=_cparams(("parallel",)),
        name="merge_outproj",
    )(*outs, *lses, w, x)


def _out_kernel(o_ref, w_ref, x_ref, out_ref, *, tiled):
    res = x_ref[...] + jnp.dot(o_ref[...].astype(BF16), w_ref[...], preferred_element_type=F32)
    if tiled:
        _to_tiles(out_ref, res)
    else:
        out_ref[...] = res


def _out_proj(o, w, x, *, tiled):
    m, d = x.shape
    tm = min(m, 1024)
    if tiled:
        out_spec = pl.BlockSpec((tm, D_TILES, LANES), lambda i: (i, 0, 0))
        out_shape = jax.ShapeDtypeStruct((m, D_TILES, LANES), F32)
    else:
        out_spec = pl.BlockSpec((tm, d), lambda i: (i, 0))
        out_shape = jax.ShapeDtypeStruct((m, d), F32)
    return pl.pallas_call(
        functools.partial(_out_kernel, tiled=tiled),
        grid=(m // tm,),
        in_specs=[pl.BlockSpec((tm, d), lambda i: (i, 0)), pl.BlockSpec((d, d), lambda i: (0, 0)),
                  pl.BlockSpec((tm, d), lambda i: (i, 0))],
        out_specs=out_spec,
        out_shape=out_shape,
        compiler_params=_cparams(("parallel",)),
        name="outproj",
    )(o, w, x)


def _silu(g):
    return g * (1.0 / (1.0 + jnp.exp(-g)))


def _ffn_kernel(x_ref, g_ref, wgu_ref, wd_ref, out_ref, *, d_ff, chunk):
    x = x_ref[...]
    h = _rms(x, g_ref[...]).astype(BF16)
    acc = jnp.zeros(x.shape, F32)
    for c in range(d_ff // chunk):
        gate = jnp.dot(h, wgu_ref[:, c * chunk:(c + 1) * chunk], preferred_element_type=F32)
        up = jnp.dot(h, wgu_ref[:, d_ff + c * chunk:d_ff + (c + 1) * chunk], preferred_element_type=F32)
        act = (_silu(gate) * up).astype(BF16)
        acc = acc + jnp.dot(act, wd_ref[c * chunk:(c + 1) * chunk, :], preferred_element_type=F32)
    out_ref[...] = x + acc


def _ffn_dense(x, g, wgu, wd):
    m, d = x.shape
    tm = min(m, 512)
    d_ff = wd.shape[0]
    kern = functools.partial(_ffn_kernel, d_ff=d_ff, chunk=d_ff // 2)
    return pl.pallas_call(
        kern,
        grid=(m // tm,),
        in_specs=[pl.BlockSpec((tm, d), lambda i: (i, 0)), pl.BlockSpec((1, d), lambda i: (0, 0)),
                  pl.BlockSpec((d, 2 * d_ff), lambda i: (0, 0), pipeline_mode=pl.Buffered(1)),
                  pl.BlockSpec((d_ff, d), lambda i: (0, 0), pipeline_mode=pl.Buffered(1))],
        out_specs=pl.BlockSpec((tm, d), lambda i: (i, 0)),
        out_shape=jax.ShapeDtypeStruct((m, d), F32),
        compiler_params=_cparams(("parallel",)),
        name="ffn_dense",
    )(x, g.reshape(1, d), wgu, wd)


def _router_kernel(x_ref, g_ref, wr_ref, idx_ref, gate_ref):
    h = _rms(_from_tiles(x_ref), g_ref[...])
    logits = jnp.dot(h, wr_ref[...], preferred_element_type=F32, precision=lax.Precision.HIGHEST)
    lane = lax.broadcasted_iota(jnp.int32, logits.shape, 1)
    logits = jnp.where(lane < N_EXPERTS, logits, -jnp.inf)
    t1 = jnp.max(logits, axis=-1, keepdims=True)
    i1 = jnp.min(jnp.where(logits == t1, lane, LANES), axis=-1, keepdims=True)
    rest = jnp.where(lane == i1, -jnp.inf, logits)
    t2 = jnp.max(rest, axis=-1, keepdims=True)
    i2 = jnp.min(jnp.where(rest == t2, lane, LANES), axis=-1, keepdims=True)
    e2 = jnp.exp(t2 - t1)
    g1 = 1.0 / (1.0 + e2)
    g2 = e2 / (1.0 + e2)
    idx_ref[...] = jnp.where(lane == 0, i1, jnp.where(lane == 1, i2, 0))
    gate_ref[...] = jnp.where(lane == 0, g1, jnp.where(lane == 1, g2, 0.0))


def _router(x3, g, w_router):
    m = x3.shape[0]
    d = D_MODEL
    tm = min(m, 1024)
    wr = jnp.zeros((d, LANES), F32).at[:, :N_EXPERTS].set(w_router)
    idx, gate = pl.pallas_call(
        _router_kernel,
        grid=(m // tm,),
        in_specs=[pl.BlockSpec((tm, D_TILES, LANES), lambda i: (i, 0, 0)), pl.BlockSpec((1, d), lambda i: (0, 0)),
                  pl.BlockSpec((d, LANES), lambda i: (0, 0))],
        out_specs=[pl.BlockSpec((tm, LANES), lambda i: (i, 0))] * 2,
        out_shape=[jax.ShapeDtypeStruct((m, LANES), jnp.int32), jax.ShapeDtypeStruct((m, LANES), F32)],
        compiler_params=_cparams(("parallel",)),
        name="router_top2",
    )(x3, g.reshape(1, d), wr)
    return idx[:, :2], gate[:, :2]


ROW_UNROLL = 8


def _row_copy(src_hbm, dst, sem, src_row, dst_row):
    return pltpu.make_async_copy(src_hbm.at[pl.ds(src_row, 1)], dst.at[pl.ds(dst_row, 1)], sem)


def _gather_kernel(idx_ref, x_hbm, o_ref, sem, *, rows):
    def start(t, carry):
        for u in range(ROW_UNROLL):
            r = t * ROW_UNROLL + u
            _row_copy(x_hbm, o_ref, sem, idx_ref[0, 0, r], r).start(priority=u % 2)
        return carry

    def wait(t, carry):
        for u in range(ROW_UNROLL):
            _row_copy(x_hbm, o_ref, sem, 0, t * ROW_UNROLL + u).wait()
        return carry

    lax.fori_loop(0, rows // ROW_UNROLL, start, 0)
    lax.fori_loop(0, rows // ROW_UNROLL, wait, 0)


def _gather_rows(x3, idx, rows):
    n = idx.shape[0]
    return pl.pallas_call(
        functools.partial(_gather_kernel, rows=rows),
        grid=(n // rows,),
        in_specs=[pl.BlockSpec((1, 1, rows), lambda i: (i, 0, 0), memory_space=pltpu.SMEM),
                  pl.BlockSpec(memory_space=pl.ANY)],
        out_specs=pl.BlockSpec((rows, D_TILES, LANES), lambda i: (i, 0, 0)),
        scratch_shapes=[pltpu.SemaphoreType.DMA(())],
        out_shape=jax.ShapeDtypeStruct((n, D_TILES, LANES), x3.dtype),
        compiler_params=_cparams(("arbitrary",)),
        name="moe_gather",
    )(idx.reshape(n // rows, 1, rows), x3)


def _expert_kernel(te_ref, nv_ref, xs_ref, g_ref, gate_ref, wgu_ref, wd_ref, o_ref, *, d_ff):
    @pl.when(pl.program_id(0) < nv_ref[0])
    def _():
        h = _rms(_from_tiles(xs_ref), g_ref[...]).astype(BF16)
        gate = jnp.dot(h, wgu_ref[0, :, :d_ff], preferred_element_type=F32)
        up = jnp.dot(h, wgu_ref[0, :, d_ff:], preferred_element_type=F32)
        act = (_silu(gate) * up).astype(BF16)
        y = jnp.dot(act, wd_ref[0], preferred_element_type=F32)
        _to_tiles(o_ref, y * gate_ref[:, 0:1])

    @pl.when(pl.program_id(0) >= nv_ref[0])
    def _():
        o_ref[...] = jnp.zeros(o_ref.shape, o_ref.dtype)


def _experts(xs, g, gate_rows, tile_expert, n_valid, wgu, wd, tm):
    n = xs.shape[0]
    d = D_MODEL
    d_ff = wd.shape[1]
    return pl.pallas_call(
        functools.partial(_expert_kernel, d_ff=d_ff),
        grid_spec=pltpu.PrefetchScalarGridSpec(
            num_scalar_prefetch=2,
            grid=(n // tm,),
            in_specs=[
                pl.BlockSpec((tm, D_TILES, LANES), lambda i, te, nv: (i, 0, 0)),
                pl.BlockSpec((1, d), lambda i, te, nv: (0, 0)),
                pl.BlockSpec((tm, LANES), lambda i, te, nv: (i, 0)),
                pl.BlockSpec((1, d, 2 * d_ff), lambda i, te, nv: (te[i], 0, 0)),
                pl.BlockSpec((1, d_ff, d), lambda i, te, nv: (te[i], 0, 0)),
            ],
            out_specs=pl.BlockSpec((tm, D_TILES, LANES), lambda i, te, nv: (i, 0, 0)),
        ),
        out_shape=jax.ShapeDtypeStruct((n, D_TILES, LANES), F32),
        compiler_params=_cparams(("arbitrary",)),
        name="moe_experts",
    )(tile_expert, n_valid, xs, g.reshape(1, d), gate_rows, wgu, wd)


def _combine_kernel(pos_ref, x_ref, y_hbm, o_ref, buf0, buf1, sem, *, rows):
    def start(t, carry):
        for u in range(ROW_UNROLL // 2):
            r = t * (ROW_UNROLL // 2) + u
            _row_copy(y_hbm, buf0, sem.at[0], pos_ref[0, 0, 2 * r], r).start(priority=0)
            _row_copy(y_hbm, buf1, sem.at[1], pos_ref[0, 0, 2 * r + 1], r).start(priority=1)
        return carry

    def wait(t, carry):
        for u in range(ROW_UNROLL // 2):
            r = t * (ROW_UNROLL // 2) + u
            _row_copy(y_hbm, buf0, sem.at[0], 0, r).wait()
            _row_copy(y_hbm, buf1, sem.at[1], 0, r).wait()
        return carry

    lax.fori_loop(0, rows // (ROW_UNROLL // 2), start, 0)
    lax.fori_loop(0, rows // (ROW_UNROLL // 2), wait, 0)
    for c in range(D_TILES):
        o_ref[:, c * LANES:(c + 1) * LANES] = x_ref[:, c, :] + (buf0[:, c, :] + buf1[:, c, :])


def _combine(x3, ys, pos, rows):
    m = x3.shape[0]
    tile = (rows, D_TILES, LANES)
    return pl.pallas_call(
        functools.partial(_combine_kernel, rows=rows),
        grid=(m // rows,),
        in_specs=[pl.BlockSpec((1, 1, 2 * rows), lambda i: (i, 0, 0), memory_space=pltpu.SMEM),
                  pl.BlockSpec(tile, lambda i: (i, 0, 0)), pl.BlockSpec(memory_space=pl.ANY)],
        out_specs=pl.BlockSpec((rows, D_MODEL), lambda i: (i, 0)),
        scratch_shapes=[pltpu.VMEM(tile, F32), pltpu.VMEM(tile, F32), pltpu.SemaphoreType.DMA((2,))],
        out_shape=jax.ShapeDtypeStruct((m, D_MODEL), F32),
        compiler_params=_cparams(("arbitrary",)),
        name="moe_combine",
    )(pos.reshape(m // rows, 1, 2 * rows), x3, ys)


def _moe(x3, g, w_router, wgu, wd, tm):
    m = x3.shape[0]
    top_idx, top_gate = _router(x3, g, w_router)
    e_flat = top_idx.reshape(-1)
    onehot = (e_flat[:, None] == jnp.arange(N_EXPERTS)[None, :]).astype(jnp.int32)
    counts = jnp.sum(onehot, axis=0)
    rank = jnp.sum((jnp.cumsum(onehot, axis=0) - onehot) * onehot, axis=1)
    padded = ((counts + tm - 1) // tm) * tm
    ends = jnp.cumsum(padded)
    pos = (ends - padded)[e_flat] + rank
    n_tiles = -(-(2 * m + N_EXPERTS * (tm - 1)) // tm)
    n_rows = n_tiles * tm
    src_token = jnp.zeros((n_rows,), jnp.int32).at[pos].set(jnp.arange(2 * m, dtype=jnp.int32) // 2)
    gate_rows = jnp.zeros((n_rows,), F32).at[pos].set(top_gate.reshape(-1))
    gate_rows = jnp.broadcast_to(gate_rows[:, None], (n_rows, LANES))
    tile_start = jnp.arange(n_tiles, dtype=jnp.int32) * tm
    tile_expert = jnp.minimum(jnp.sum((tile_start[:, None] >= ends[None, :]).astype(jnp.int32), axis=1),
                              N_EXPERTS - 1).astype(jnp.int32)
    n_valid = (ends[-1] // tm).astype(jnp.int32).reshape(1)

    xs = _gather_rows(x3, src_token, tm)
    ys = _experts(xs, g, gate_rows, tile_expert, n_valid, wgu, wd, tm)
    return _combine(x3, ys, pos.astype(jnp.int32), min(m, 256))


def _step_softmax(s, s_new, sink):
    m = jnp.maximum(jnp.max(s, axis=0), s_new)
    if sink is not None:
        m = jnp.maximum(m, sink)
    p = jnp.exp(s - m[None])
    p_new = jnp.exp(s_new - m)
    den = jnp.sum(p, axis=0) + p_new
    if sink is not None:
        den = den + jnp.exp(sink - m)
    return p, p_new, den, m + jnp.log(den)


def _step_a_kernel(q_ref, c0_ref, c1_ref, c2_ref, slope_ref, o_ref):
    steps = (N_STEPS - lax.broadcasted_iota(jnp.int32, (N_STEPS, 1, 1), 0)).astype(F32)
    slope = slope_ref[...]
    outs, lses = [], []
    for g, (c_ref, (_, dil)) in enumerate(zip((c0_ref, c1_ref, c2_ref), A_GROUPS)):
        q, k_new, v_new = q_ref[3 * g], q_ref[3 * g + 1], q_ref[3 * g + 2]
        kk, vv = c_ref[:, 0], c_ref[:, 1]
        s = jnp.sum(kk * q[None], axis=-1, keepdims=True) - (steps * float(dil)) * slope[None]
        s_new = jnp.sum(k_new * q, axis=-1, keepdims=True)
        p, p_new, den, lse = _step_softmax(s, s_new, None)
        outs.append((jnp.sum(p * vv, axis=0) + p_new * v_new) * (1.0 / den))
        lses.append(lse)
    top = jnp.maximum(jnp.maximum(lses[0], lses[1]), lses[2])
    es = [jnp.exp(l - top) for l in lses]
    o_ref[...] = (es[0] * outs[0] + es[1] * outs[1] + es[2] * outs[2]) * (1.0 / (es[0] + es[1] + es[2]))


def _step_attention_a(qkv, caches, li):
    n = qkv.shape[0]
    views = [c.reshape(c.shape[0], n, N_STEPS, dil, 2, N_HEADS, HEAD_DIM) for c, (_, dil) in zip(caches, A_GROUPS)]
    slope = jnp.asarray(SLOPES, F32).reshape(N_HEADS, 1)
    cache_spec = pl.BlockSpec((None, None, N_STEPS, None, 2, N_HEADS, HEAD_DIM), lambda i: (li, i, 0, 0, 0, 0, 0))
    return pl.pallas_call(
        _step_a_kernel,
        grid=(n,),
        in_specs=[pl.BlockSpec((None, 9, N_HEADS, HEAD_DIM), lambda i: (i, 0, 0, 0))] + [cache_spec] * 3
        + [pl.BlockSpec((N_HEADS, 1), lambda i: (0, 0))],
        out_specs=pl.BlockSpec((None, N_HEADS, HEAD_DIM), lambda i: (i, 0, 0)),
        out_shape=jax.ShapeDtypeStruct((n, N_HEADS, HEAD_DIM), F32),
        compiler_params=_cparams(("parallel",)),
        name="step_attn_a",
    )(qkv, *views, slope)


def _step_b_kernel(q_ref, kn_ref, vn_ref, c_ref, slope_ref, sink_ref, o_ref):
    steps = (N_STEPS - lax.broadcasted_iota(jnp.int32, (N_STEPS, 1), 0)).astype(F32)
    rep = N_HEADS // HKV_B
    for kv in range(HKV_B):
        kk, vv = c_ref[:, 0, kv, :], c_ref[:, 1, kv, :]
        k_new, v_new = kn_ref[kv:kv + 1, :], vn_ref[kv:kv + 1, :]
        for h in range(kv * rep, (kv + 1) * rep):
            q = q_ref[h:h + 1, :]
            slope, sink = slope_ref[h:h + 1, :], sink_ref[h:h + 1, :]
            s = jnp.sum(kk * q, axis=-1, keepdims=True) - steps * slope
            s_new = jnp.sum(k_new * q, axis=-1, keepdims=True)
            m = jnp.maximum(jnp.maximum(jnp.max(s, axis=0, keepdims=True), s_new), sink)
            p = jnp.exp(s - m)
            p_new = jnp.exp(s_new - m)
            den = jnp.sum(p, axis=0, keepdims=True) + p_new + jnp.exp(sink - m)
            o_ref[h:h + 1, :] = (jnp.sum(p * vv, axis=0, keepdims=True) + p_new * v_new) * (1.0 / den)


def _step_attention_b(q, k_new, v_new, cache, sink, li):
    n = q.shape[0]
    col = lambda v: v.astype(F32).reshape(N_HEADS, 1)
    return pl.pallas_call(
        _step_b_kernel,
        grid=(n,),
        in_specs=[pl.BlockSpec((None, N_HEADS, HEAD_DIM), lambda i: (i, 0, 0)),
                  pl.BlockSpec((None, HKV_B, HEAD_DIM), lambda i: (i, 0, 0)),
                  pl.BlockSpec((None, HKV_B, HEAD_DIM), lambda i: (i, 0, 0)),
                  pl.BlockSpec((None, None, N_STEPS, 2, HKV_B, HEAD_DIM), lambda i: (li, i, 0, 0, 0, 0)),
                  pl.BlockSpec((N_HEADS, 1), lambda i: (0, 0)), pl.BlockSpec((N_HEADS, 1), lambda i: (0, 0))],
        out_specs=pl.BlockSpec((None, N_HEADS, HEAD_DIM), lambda i: (i, 0, 0)),
        out_shape=jax.ShapeDtypeStruct((n, N_HEADS, HEAD_DIM), F32),
        compiler_params=_cparams(("parallel",)),
        name="step_attn_b",
    )(q, k_new, v_new, cache, col(jnp.asarray(SLOPES, F32)), col(sink))


def _tile_heads(v):
    return jnp.tile(v.astype(F32), N_HEADS)


def _kept_rows(qkv, keep, k_col, width, heads):
    batch, dil, ls, _ = qkv.shape
    kv = qkv[:, :, ls - keep // dil:, k_col:k_col + 2 * width]
    kv = kv.transpose(0, 2, 1, 3).reshape(batch, keep, 2, heads, HEAD_DIM)
    return kv.astype(F32)


def kernel(x_prompt, x_sample, cache_a_w128, cache_a_w512, cache_a_w2048, cache_b, norm_mix_a, w_in_a, q_gain_a, k_gain_a, w_out_a, norm_ffn_dense, w_gu_dense, w_down_dense, norm_mix_b, w_in_b, q_gain_b, k_gain_b, sink_b, w_out_b, norm_ffn_moe, w_router, w_gu_moe, w_down_moe):
    batch, s_len, d = x_prompt.shape
    n_dec = x_sample.shape[0]
    assert x_sample.shape[1] == 1 and d == D_MODEL
    caches_a = (cache_a_w128, cache_a_w512, cache_a_w2048)
    xp = x_prompt.reshape(batch * s_len, d)
    xs = x_sample.reshape(n_dec, d)
    q_scale = HEAD_DIM ** -0.5
    n_grp = len(A_GROUPS)
    outs = []

    li = 0
    w_in = w_in_a[li].astype(BF16)
    cols_a = w_in.shape[1]
    gain = jnp.concatenate([jnp.concatenate([_tile_heads(q_gain_a[li, g]) * q_scale, _tile_heads(k_gain_a[li, g]),
                                             jnp.ones((d,), F32)]) for g in range(n_grp)]).reshape(1, cols_a)
    flag = jnp.tile(jnp.concatenate([jnp.ones((2 * d,), F32), jnp.zeros((d,), F32)]), n_grp).reshape(1, cols_a)
    qkv_s = _proj(xs, norm_mix_a[li], w_in, gain, flag, batch=1, dil=1, tn=d, col0=0, ncols=cols_a)
    qkv_s = qkv_s.reshape(n_dec, cols_a)
    no_sink = jnp.zeros((N_HEADS,), F32)
    o_p, l_p = [], []
    for g, (window, dil) in enumerate(A_GROUPS):
        qkv_g = _proj(xp, norm_mix_a[li], w_in, gain, flag, batch=batch, dil=dil, tn=d, col0=3 * g, ncols=3 * d)
        o, l = _band_attention(qkv_g, no_sink, q_blk=0, k_blk=1, v_blk=2, kv_heads=N_HEADS, has_sink=False)
        o_p.append(o)
        l_p.append(l)
        outs.append(_kept_rows(qkv_g, min(window, s_len), d, d, N_HEADS)[None])
        kv_s = qkv_s[:, (3 * g + 1) * d:(3 * g + 3) * d]
        outs.append(kv_s.reshape(1, n_dec, 1, 2, N_HEADS, HEAD_DIM).astype(F32))
    w_out = w_out_a[li].astype(BF16)
    xp = _merge_out(o_p, l_p, w_out, xp)
    o_s = _step_attention_a(qkv_s.reshape(n_dec, 3 * n_grp, N_HEADS, HEAD_DIM).astype(F32), caches_a, li)
    xs = _out_proj(o_s.reshape(n_dec, d), w_out, xs, tiled=False)
    wgu, wd = w_gu_dense[li].astype(BF16), w_down_dense[li].astype(BF16)
    xp = _ffn_dense(xp, norm_ffn_dense[li], wgu, wd)
    xs = _ffn_dense(xs, norm_ffn_dense[li], wgu, wd)

    w_in = w_in_b[li].astype(BF16)
    cols_b = w_in.shape[1]
    kvw = HKV_B * HEAD_DIM
    gain = jnp.concatenate([_tile_heads(q_gain_b[li]) * q_scale, jnp.tile(k_gain_b[li].astype(F32), HKV_B),
                            jnp.ones((kvw,), F32)]).reshape(1, cols_b)
    flag = jnp.concatenate([jnp.ones((d + kvw,), F32), jnp.zeros((kvw,), F32)]).reshape(1, cols_b)
    qkv_p = _proj(xp, norm_mix_b[li], w_in, gain, flag, batch=batch, dil=1, tn=cols_b, col0=0, ncols=cols_b)
    qkv_s = _proj(xs, norm_mix_b[li], w_in, gain, flag, batch=1, dil=1, tn=cols_b, col0=0, ncols=cols_b)
    qkv_s = qkv_s.reshape(n_dec, cols_b).astype(F32)
    sink = sink_b[li].astype(F32)
    o, _ = _band_attention(qkv_p, sink, q_blk=0, k_blk=d // kvw, v_blk=d // kvw + 1, kv_heads=HKV_B, has_sink=True)
    outs.append(_kept_rows(qkv_p, min(N_STEPS, s_len), d, kvw, HKV_B)[None])
    outs.append(qkv_s[:, d:].reshape(1, n_dec, 1, 2, HKV_B, HEAD_DIM))
    w_out = w_out_b[li].astype(BF16)
    xp3 = _out_proj(o.reshape(batch * s_len, d), w_out, xp, tiled=True)
    o_s = _step_attention_b(qkv_s[:, :d].reshape(n_dec, N_HEADS, HEAD_DIM),
                            qkv_s[:, d:d + kvw].reshape(n_dec, HKV_B, HEAD_DIM),
                            qkv_s[:, d + kvw:].reshape(n_dec, HKV_B, HEAD_DIM), cache_b, sink, li)
    xs3 = _out_proj(o_s.reshape(n_dec, d), w_out, xs, tiled=True)
    wgu, wd = w_gu_moe[li].astype(BF16), w_down_moe[li].astype(BF16)
    xp = _moe(xp3, norm_ffn_moe[li], w_router[li], wgu, wd, 512)
    xs = _moe(xs3, norm_ffn_moe[li], w_router[li], wgu, wd, 128)

    return (xp.reshape(batch, s_len, d), xs.reshape(n_dec, 1, d), *outs)
```

```python
import functools

import numpy as np
import jax
import jax.numpy as jnp
from jax import lax
from jax.experimental import pallas as pl
from jax.experimental.pallas import tpu as pltpu

F32 = jnp.float32
BF16 = jnp.bfloat16

D_MODEL = 1024
HEAD_DIM = 64
N_HEADS = 16
HKV_B = 2
A_GROUPS = ((128, 1), (512, 4), (2048, 16))
N_STEPS = 128
BLK = 128
N_EXPERTS = 8
RMS_EPS = 1e-6
MASKED = -1e30
LANES = 128
SUBLANES = 8
MXU_DIM = 256
VMEM_LIMIT = 56 * 1024 * 1024
D_TILES = D_MODEL // LANES

SLOPES = tuple(float(2.0 ** (-8.0 * (h + 1) / N_HEADS)) for h in range(N_HEADS))


def _cparams(sem):
    return pltpu.CompilerParams(dimension_semantics=sem, vmem_limit_bytes=VMEM_LIMIT)


def _rms(x, g):
    ms = jnp.mean(x * x, axis=-1, keepdims=True)
    return x * lax.rsqrt(ms + RMS_EPS) * g


def _from_tiles(ref):
    return jnp.concatenate([ref[:, c, :] for c in range(D_TILES)], axis=1)


def _to_tiles(ref, val):
    for c in range(D_TILES):
        ref[:, c, :] = val[:, c * LANES:(c + 1) * LANES]


def _proj_kernel(x_ref, g_ref, w_ref, gain_ref, flag_ref, seg_ref, o_ref, h_scr, *res_scr, dil):
    @pl.when(pl.program_id(1) == 0)
    def _():
        h_scr[...] = _rms(x_ref[...], g_ref[...]).astype(BF16)

    acc = jnp.dot(h_scr[...], w_ref[...], preferred_element_type=F32)
    sq = (acc * acc).astype(BF16)
    tm, tn = acc.shape
    ssq = jnp.concatenate(
        [jnp.dot(sq[:, c * MXU_DIM:(c + 1) * MXU_DIM], seg_ref[...], preferred_element_type=F32)
         for c in range(tn // MXU_DIM)], axis=1)
    nrm = acc * lax.rsqrt(ssq * (1.0 / HEAD_DIM) + RMS_EPS) * gain_ref[...]
    res = jnp.where(flag_ref[...] > 0.0, nrm, acc)
    if dil == 1:
        o_ref[0, 0] = res.astype(o_ref.dtype)
    else:
        scr = res_scr[0]
        for c in range(tn // LANES):
            cols = slice(c * LANES, (c + 1) * LANES)
            scr[c] = res[:, cols]
            for r in range(dil):
                o_ref[0, r, :, cols] = scr[c, pl.ds(r, tm // dil, stride=dil), :].astype(o_ref.dtype)


def _proj(x, g, w, gain, flag, *, batch, dil, tn, col0, ncols):
    m, d = x.shape
    s_len = m // batch
    tm = min(s_len, 1024)
    tpb = s_len // tm
    seg = jnp.asarray(np.kron(np.eye(MXU_DIM // HEAD_DIM), np.ones((HEAD_DIM, HEAD_DIM))), BF16)
    scratch = [pltpu.VMEM((tm, d), BF16)] + ([pltpu.VMEM((tn // LANES, tm, LANES), F32)] if dil > 1 else [])
    return pl.pallas_call(
        functools.partial(_proj_kernel, dil=dil),
        grid=(m // tm, ncols // tn),
        in_specs=[
            pl.BlockSpec((tm, d), lambda i, j: (i, 0)),
            pl.BlockSpec((1, d), lambda i, j: (0, 0)),
            pl.BlockSpec((d, tn), lambda i, j: (0, col0 + j)),
            pl.BlockSpec((1, tn), lambda i, j: (0, col0 + j)),
            pl.BlockSpec((1, tn), lambda i, j: (0, col0 + j)),
            pl.BlockSpec((MXU_DIM, MXU_DIM), lambda i, j: (0, 0)),
        ],
        out_specs=pl.BlockSpec((1, dil, tm // dil, tn), lambda i, j: (i // tpb, 0, i % tpb, j)),
        out_shape=jax.ShapeDtypeStruct((batch, dil, s_len // dil, ncols), BF16),
        scratch_shapes=scratch,
        compiler_params=_cparams(("parallel", "arbitrary")),
        name="proj_qknorm_d%d" % dil,
    )(x, g.reshape(1, d), w, gain, flag, seg)


def _band_kernel(sink_ref, q_ref, kp_ref, kc_ref, vp_ref, vc_ref, o_ref, lse_ref, k_all, v_all, *, dil, qb,
                 kv_heads, has_sink):
    c = pl.program_id(2)
    k_all[0:BLK] = kp_ref[0, 0]
    k_all[BLK:] = kc_ref[0, 0]
    v_all[0:BLK] = vp_ref[0, 0]
    v_all[BLK:] = vc_ref[0, 0]

    qi = lax.broadcasted_iota(jnp.int32, (BLK, 2 * BLK), 0)
    sj = lax.broadcasted_iota(jnp.int32, (BLK, 2 * BLK), 1)
    dist = qi - sj + BLK
    base = jnp.where((dist >= 0) & (dist <= N_STEPS), -(dist.astype(F32) * float(dil)), MASKED)
    base_first = jnp.where(sj >= BLK, base, MASKED)
    low_k = lax.broadcasted_iota(jnp.int32, (2 * BLK, LANES), 1) < HEAD_DIM
    low_q = lax.broadcasted_iota(jnp.int32, (BLK, LANES), 1) < HEAD_DIM
    zeros_k = jnp.zeros((2 * BLK, LANES), BF16)
    ones_a = jnp.where(low_k, 1.0, 0.0).astype(BF16)
    ones_b = jnp.where(low_k, 0.0, 1.0).astype(BF16)
    nt = (((1,), (1,)), ((), ()))

    for i in range(qb):
        rows = slice(i * BLK, (i + 1) * BLK)
        win = slice(i * BLK, (i + 2) * BLK)
        base_i = jnp.where(c == 0, base_first, base) if i == 0 else base
        if kv_heads != N_HEADS:
            kw, vw = k_all[win, :], v_all[win, :]
            zero_half = jnp.zeros((2 * BLK, HEAD_DIM), BF16)
            shared = {}
            for kv in range(kv_heads):
                k_kv = kw[:, kv * HEAD_DIM:(kv + 1) * HEAD_DIM]
                v_kv = vw[:, kv * HEAD_DIM:(kv + 1) * HEAD_DIM]
                shared[kv] = tuple((jnp.concatenate([t, zero_half], axis=1), jnp.concatenate([zero_half, t], axis=1))
                                   for t in (k_kv, v_kv))
        for pair in range(N_HEADS // 2):
            pc = slice(pair * LANES, (pair + 1) * LANES)
            ha, hb = 2 * pair, 2 * pair + 1
            qp = q_ref[0, 0, rows, pc]
            if kv_heads == N_HEADS:
                kw, vw = k_all[win, pc], v_all[win, pc]
                k_a, k_b = jnp.where(low_k, kw, zeros_k), jnp.where(low_k, zeros_k, kw)
                v_a, v_b = jnp.where(low_k, vw, zeros_k), jnp.where(low_k, zeros_k, vw)
            else:
                (k_a, k_b), (v_a, v_b) = shared[ha // (N_HEADS // kv_heads)]
            s_a = lax.dot_general(qp, k_a, nt, preferred_element_type=F32) + SLOPES[ha] * base_i
            s_b = lax.dot_general(qp, k_b, nt, preferred_element_type=F32) + SLOPES[hb] * base_i
            m_a = jnp.max(s_a, axis=-1, keepdims=True)
            m_b = jnp.max(s_b, axis=-1, keepdims=True)
            if has_sink:
                m_a = jnp.maximum(m_a, sink_ref[ha])
                m_b = jnp.maximum(m_b, sink_ref[hb])
            p_a = jnp.exp(s_a - m_a).astype(BF16)
            p_b = jnp.exp(s_b - m_b).astype(BF16)
            res = (jnp.dot(p_a, jnp.concatenate([v_a, ones_a], axis=1), preferred_element_type=F32)
                   + jnp.dot(p_b, jnp.concatenate([v_b, ones_b], axis=1), preferred_element_type=F32))
            den = res[:, LANES:]
            m_pair = jnp.where(low_q, m_a, m_b)
            if has_sink:
                den = den + jnp.where(low_q, jnp.exp(sink_ref[ha] - m_a), jnp.exp(sink_ref[hb] - m_b))
            o_ref[0, 0, rows, pc] = (res[:, :LANES] * (1.0 / den)).astype(o_ref.dtype)
            lse_ref[0, 0, rows, pc] = m_pair + jnp.log(den)


def _band_attention(qkv, sink, *, q_blk, k_blk, v_blk, kv_heads, has_sink):
    batch, dil, ls, _ = qkv.shape
    qb = min(4, ls // BLK)
    rows = BLK * qb
    kvw = kv_heads * HEAD_DIM
    kern = functools.partial(_band_kernel, dil=dil, qb=qb, kv_heads=kv_heads, has_sink=has_sink)
    prev = lambda b, r, c: jnp.maximum(c * qb - 1, 0)
    return pl.pallas_call(
        kern,
        grid=(batch, dil, ls // rows),
        in_specs=[
            pl.BlockSpec(memory_space=pltpu.SMEM),
            pl.BlockSpec((1, 1, rows, D_MODEL), lambda b, r, c: (b, r, c, q_blk)),
            pl.BlockSpec((1, 1, BLK, kvw), lambda b, r, c: (b, r, prev(b, r, c), k_blk)),
            pl.BlockSpec((1, 1, rows, kvw), lambda b, r, c: (b, r, c, k_blk)),
            pl.BlockSpec((1, 1, BLK, kvw), lambda b, r, c: (b, r, prev(b, r, c), v_blk)),
            pl.BlockSpec((1, 1, rows, kvw), lambda b, r, c: (b, r, c, v_blk)),
        ],
        out_specs=[pl.BlockSpec((1, 1, rows, D_MODEL), lambda b, r, c: (b, r, c, 0))] * 2,
        out_shape=[jax.ShapeDtypeStruct((batch, dil, ls, D_MODEL), BF16),
                   jax.ShapeDtypeStruct((batch, dil, ls, D_MODEL), F32)],
        scratch_shapes=[pltpu.VMEM((BLK + rows, kvw), BF16), pltpu.VMEM((BLK + rows, kvw), BF16)],
        compiler_params=_cparams(("parallel", "parallel", "arbitrary")),
        name="band_attn_d%d" % dil,
    )(sink, qkv, qkv, qkv, qkv, qkv)


def _merge_out_kernel(*refs, dils):
    o_refs, l_refs = refs[0:3], refs[3:6]
    w_ref, x_ref, out_ref = refs[6:9]
    scr = refs[9:]
    tm = x_ref.shape[0]
    outs, lses = [], []
    k = 0
    for o_ref, l_ref, dil in zip(o_refs, l_refs, dils):
        if dil == 1:
            outs.append(o_ref[0, 0].astype(F32))
            lses.append(l_ref[0, 0])
        else:
            so, sl = scr[k], scr[k + 1]
            k += 2
            for c in range(D_TILES):
                cols = slice(c * LANES, (c + 1) * LANES)
                for r in range(dil):
                    so[c, pl.ds(r, tm // dil, stride=dil), :] = o_ref[0, r, :, cols].astype(F32)
                    sl[c, pl.ds(r, tm // dil, stride=dil), :] = l_ref[0, r, :, cols]
            outs.append(jnp.concatenate([so[c] for c in range(D_TILES)], axis=1))
            lses.append(jnp.concatenate([sl[c] for c in range(D_TILES)], axis=1))
    top = jnp.maximum(jnp.maximum(lses[0], lses[1]), lses[2])
    es = [jnp.exp(l - top) for l in lses]
    mix = (es[0] * outs[0] + es[1] * outs[1] + es[2] * outs[2]) * (1.0 / (es[0] + es[1] + es[2]))
    out_ref[...] = x_ref[...] + jnp.dot(mix.astype(BF16), w_ref[...], preferred_element_type=F32)


def _merge_out(outs, lses, w, x):
    m, d = x.shape
    batch = outs[0].shape[0]
    s_len = m // batch
    tm = min(s_len, 256)
    tpb = s_len // tm
    dils = tuple(o.shape[1] for o in outs)
    grp = [pl.BlockSpec((1, dil, tm // dil, d), lambda i: (i // tpb, 0, i % tpb, 0)) for dil in dils]
    scratch = [pltpu.VMEM((D_TILES, tm, LANES), F32) for dil in dils if dil > 1 for _ in range(2)]
    return pl.pallas_call(
        functools.partial(_merge_out_kernel, dils=dils),
        grid=(m // tm,),
        in_specs=grp + grp + [pl.BlockSpec((d, d), lambda i: (0, 0)), pl.BlockSpec((tm, d), lambda i: (i, 0))],
        out_specs=pl.BlockSpec((tm, d), lambda i: (i, 0)),
        out_shape=jax.ShapeDtypeStruct((m, d), F32),
        scratch_shapes=scratch,
        compiler_params=_cparams(("parallel",)),
        name="merge_outproj",
    )(*outs, *lses, w, x)


def _out_kernel(o_ref, w_ref, x_ref, out_ref, *, tiled):
    res = x_ref[...] + jnp.dot(o_ref[...].astype(BF16), w_ref[...], preferred_element_type=F32)
    if tiled:
        _to_tiles(out_ref, res)
    else:
        out_ref[...] = res


def _out_proj(o, w, x, *, tiled):
    m, d = x.shape
    tm = min(m, 1024)
    if tiled:
        out_spec = pl.BlockSpec((tm, D_TILES, LANES), lambda i: (i, 0, 0))
        out_shape = jax.ShapeDtypeStruct((m, D_TILES, LANES), F32)
    else:
        out_spec = pl.BlockSpec((tm, d), lambda i: (i, 0))
        out_shape = jax.ShapeDtypeStruct((m, d), F32)
    return pl.pallas_call(
        functools.partial(_out_kernel, tiled=tiled),
        grid=(m // tm,),
        in_specs=[pl.BlockSpec((tm, d), lambda i: (i, 0)), pl.BlockSpec((d, d), lambda i: (0, 0)),
                  pl.BlockSpec((tm, d), lambda i: (i, 0))],
        out_specs=out_spec,
        out_shape=out_shape,
        compiler_params=_cparams(("parallel",)),
        name="outproj",
    )(o, w, x)


def _silu(g):
    return g * (1.0 / (1.0 + jnp.exp(-g)))


def _ffn_kernel(x_ref, g_ref, wgu_ref, wd_ref, out_ref, *, d_ff, chunk):
    x = x_ref[...]
    h = _rms(x, g_ref[...]).astype(BF16)
    acc = jnp.zeros(x.shape, F32)
    for c in range(d_ff // chunk):
        gate = jnp.dot(h, wgu_ref[:, c * chunk:(c + 1) * chunk], preferred_element_type=F32)
        up = jnp.dot(h, wgu_ref[:, d_ff + c * chunk:d_ff + (c + 1) * chunk], preferred_element_type=F32)
        act = (_silu(gate) * up).astype(BF16)
        acc = acc + jnp.dot(act, wd_ref[c * chunk:(c + 1) * chunk, :], preferred_element_type=F32)
    out_ref[...] = x + acc


def _ffn_dense(x, g, wgu, wd):
    m, d = x.shape
    tm = min(m, 512)
    d_ff = wd.shape[0]
    kern = functools.partial(_ffn_kernel, d_ff=d_ff, chunk=d_ff // 2)
    return pl.pallas_call(
        kern,
        grid=(m // tm,),
        in_specs=[pl.BlockSpec((tm, d), lambda i: (i, 0)), pl.BlockSpec((1, d), lambda i: (0, 0)),
                  pl.BlockSpec((d, 2 * d_ff), lambda i: (0, 0), pipeline_mode=pl.Buffered(1)),
                  pl.BlockSpec((d_ff, d), lambda i: (0, 0), pipeline_mode=pl.Buffered(1))],
        out_specs=pl.BlockSpec((tm, d), lambda i: (i, 0)),
        out_shape=jax.ShapeDtypeStruct((m, d), F32),
        compiler_params=_cparams(("parallel",)),
        name="ffn_dense",
    )(x, g.reshape(1, d), wgu, wd)


def _router_kernel(x_ref, g_ref, wr_ref, idx_ref, gate_ref):
    h = _rms(_from_tiles(x_ref), g_ref[...])
    logits = jnp.dot(h, wr_ref[...], preferred_element_type=F32, precision=lax.Precision.HIGHEST)
    lane = lax.broadcasted_iota(jnp.int32, logits.shape, 1)
    logits = jnp.where(lane < N_EXPERTS, logits, -jnp.inf)
    t1 = jnp.max(logits, axis=-1, keepdims=True)
    i1 = jnp.min(jnp.where(logits == t1, lane, LANES), axis=-1, keepdims=True)
    rest = jnp.where(lane == i1, -jnp.inf, logits)
    t2 = jnp.max(rest, axis=-1, keepdims=True)
    i2 = jnp.min(jnp.where(rest == t2, lane, LANES), axis=-1, keepdims=True)
    e2 = jnp.exp(t2 - t1)
    g1 = 1.0 / (1.0 + e2)
    g2 = e2 / (1.0 + e2)
    idx_ref[...] = jnp.where(lane == 0, i1, jnp.where(lane == 1, i2, 0))
    gate_ref[...] = jnp.where(lane == 0, g1, jnp.where(lane == 1, g2, 0.0))


def _router(x3, g, w_router):
    m = x3.shape[0]
    d = D_MODEL
    tm = min(m, 1024)
    wr = jnp.zeros((d, LANES), F32).at[:, :N_EXPERTS].set(w_router)
    idx, gate = pl.pallas_call(
        _router_kernel,
        grid=(m // tm,),
        in_specs=[pl.BlockSpec((tm, D_TILES, LANES), lambda i: (i, 0, 0)), pl.BlockSpec((1, d), lambda i: (0, 0)),
                  pl.BlockSpec((d, LANES), lambda i: (0, 0))],
        out_specs=[pl.BlockSpec((tm, LANES), lambda i: (i, 0))] * 2,
        out_shape=[jax.ShapeDtypeStruct((m, LANES), jnp.int32), jax.ShapeDtypeStruct((m, LANES), F32)],
        compiler_params=_cparams(("parallel",)),
        name="router_top2",
    )(x3, g.reshape(1, d), wr)
    return idx[:, :2], gate[:, :2]


ROW_UNROLL = 8


def _row_copy(src_hbm, dst, sem, src_row, dst_row):
    return pltpu.make_async_copy(src_hbm.at[pl.ds(src_row, 1)], dst.at[pl.ds(dst_row, 1)], sem)


def _gather_kernel(idx_ref, x_hbm, o_ref, sem, *, rows):
    def start(t, carry):
        for u in range(ROW_UNROLL):
            r = t * ROW_UNROLL + u
            _row_copy(x_hbm, o_ref, sem, idx_ref[0, 0, r], r).start(priority=u % 2)
        return carry

    def wait(t, carry):
        for u in range(ROW_UNROLL):
            _row_copy(x_hbm, o_ref, sem, 0, t * ROW_UNROLL + u).wait()
        return carry

    lax.fori_loop(0, rows // ROW_UNROLL, start, 0)
    lax.fori_loop(0, rows // ROW_UNROLL, wait, 0)


def _gather_rows(x3, idx, rows):
    n = idx.shape[0]
    return pl.pallas_call(
        functools.partial(_gather_kernel, rows=rows),
        grid=(n // rows,),
        in_specs=[pl.BlockSpec((1, 1, rows), lambda i: (i, 0, 0), memory_space=pltpu.SMEM),
                  pl.BlockSpec(memory_space=pl.ANY)],
        out_specs=pl.BlockSpec((rows, D_TILES, LANES), lambda i: (i, 0, 0)),
        scratch_shapes=[pltpu.SemaphoreType.DMA(())],
        out_shape=jax.ShapeDtypeStruct((n, D_TILES, LANES), x3.dtype),
        compiler_params=_cparams(("arbitrary",)),
        name="moe_gather",
    )(idx.reshape(n // rows, 1, rows), x3)


def _expert_kernel(te_ref, nv_ref, xs_ref, g_ref, gate_ref, wgu_ref, wd_ref, o_ref, *, d_ff):
    @pl.when(pl.program_id(0) < nv_ref[0])
    def _():
        h = _rms(_from_tiles(xs_ref), g_ref[...]).astype(BF16)
        gate = jnp.dot(h, wgu_ref[0, :, :d_ff], preferred_element_type=F32)
        up = jnp.dot(h, wgu_ref[0, :, d_ff:], preferred_element_type=F32)
        act = (_silu(gate) * up).astype(BF16)
        y = jnp.dot(act, wd_ref[0], preferred_element_type=F32)
        _to_tiles(o_ref, y * gate_ref[:, 0:1])

    @pl.when(pl.program_id(0) >= nv_ref[0])
    def _():
        o_ref[...] = jnp.zeros(o_ref.shape, o_ref.dtype)


def _experts(xs, g, gate_rows, tile_expert, n_valid, wgu, wd, tm):
    n = xs.shape[0]
    d = D_MODEL
    d_ff = wd.shape[1]
    return pl.pallas_call(
        functools.partial(_expert_kernel, d_ff=d_ff),
        grid_spec=pltpu.PrefetchScalarGridSpec(
            num_scalar_prefetch=2,
            grid=(n // tm,),
            in_specs=[
                pl.BlockSpec((tm, D_TILES, LANES), lambda i, te, nv: (i, 0, 0)),
                pl.BlockSpec((1, d), lambda i, te, nv: (0, 0)),
                pl.BlockSpec((tm, LANES), lambda i, te, nv: (i, 0)),
                pl.BlockSpec((1, d, 2 * d_ff), lambda i, te, nv: (te[i], 0, 0)),
                pl.BlockSpec((1, d_ff, d), lambda i, te, nv: (te[i], 0, 0)),
            ],
            out_specs=pl.BlockSpec((tm, D_TILES, LANES), lambda i, te, nv: (i, 0, 0)),
        ),
        out_shape=jax.ShapeDtypeStruct((n, D_TILES, LANES), F32),
        compiler_params=_cparams(("arbitrary",)),
        name="moe_experts",
    )(tile_expert, n_valid, xs, g.reshape(1, d), gate_rows, wgu, wd)


def _combine_kernel(pos_ref, x_ref, y_hbm, o_ref, buf0, buf1, sem, *, rows):
    def start(t, carry):
        for u in range(ROW_UNROLL // 2):
            r = t * (ROW_UNROLL // 2) + u
            _row_copy(y_hbm, buf0, sem.at[0], pos_ref[0, 0, 2 * r], r).start(priority=0)
            _row_copy(y_hbm, buf1, sem.at[1], pos_ref[0, 0, 2 * r + 1], r).start(priority=1)
        return carry

    def wait(t, carry):
        for u in range(ROW_UNROLL // 2):
            r = t * (ROW_UNROLL // 2) + u
            _row_copy(y_hbm, buf0, sem.at[0], 0, r).wait()
            _row_copy(y_hbm, buf1, sem.at[1], 0, r).wait()
        return carry

    lax.fori_loop(0, rows // (ROW_UNROLL // 2), start, 0)
    lax.fori_loop(0, rows // (ROW_UNROLL // 2), wait, 0)
    for c in range(D_TILES):
        o_ref[:, c * LANES:(c + 1) * LANES] = x_ref[:, c, :] + (buf0[:, c, :] + buf1[:, c, :])


def _combine(x3, ys, pos, rows):
    m = x3.shape[0]
    tile = (rows, D_TILES, LANES)
    return pl.pallas_call(
        functools.partial(_combine_kernel, rows=rows),
        grid=(m // rows,),
        in_specs=[pl.BlockSpec((1, 1, 2 * rows), lambda i: (i, 0, 0), memory_space=pltpu.SMEM),
                  pl.BlockSpec(tile, lambda i: (i, 0, 0)), pl.BlockSpec(memory_space=pl.ANY)],
        out_specs=pl.BlockSpec((rows, D_MODEL), lambda i: (i, 0)),
        scratch_shapes=[pltpu.VMEM(tile, F32), pltpu.VMEM(tile, F32), pltpu.SemaphoreType.DMA((2,))],
        out_shape=jax.ShapeDtypeStruct((m, D_MODEL), F32),
        compiler_params=_cparams(("arbitrary",)),
        name="moe_combine",
    )(pos.reshape(m // rows, 1, 2 * rows), x3, ys)


def _moe(x3, g, w_router, wgu, wd, tm):
    m = x3.shape[0]
    top_idx, top_gate = _router(x3, g, w_router)
    e_flat = top_idx.reshape(-1)
    onehot = (e_flat[:, None] == jnp.arange(N_EXPERTS)[None, :]).astype(jnp.int32)
    counts = jnp.sum(onehot, axis=0)
    rank = jnp.sum((jnp.cumsum(onehot, axis=0) - onehot) * onehot, axis=1)
    padded = ((counts + tm - 1) // tm) * tm
    ends = jnp.cumsum(padded)
    pos = (ends - padded)[e_flat] + rank
    n_tiles = -(-(2 * m + N_EXPERTS * (tm - 1)) // tm)
    n_rows = n_tiles * tm
    src_token = jnp.zeros((n_rows,), jnp.int32).at[pos].set(jnp.arange(2 * m, dtype=jnp.int32) // 2)
    gate_rows = jnp.zeros((n_rows,), F32).at[pos].set(top_gate.reshape(-1))
    gate_rows = jnp.broadcast_to(gate_rows[:, None], (n_rows, LANES))
    tile_start = jnp.arange(n_tiles, dtype=jnp.int32) * tm
    tile_expert = jnp.minimum(jnp.sum((tile_start[:, None] >= ends[None, :]).astype(jnp.int32), axis=1),
                              N_EXPERTS - 1).astype(jnp.int32)
    n_valid = (ends[-1] // tm).astype(jnp.int32).reshape(1)

    xs = _gather_rows(x3, src_token, tm)
    ys = _experts(xs, g, gate_rows, tile_expert, n_valid, wgu, wd, tm)
    return _combine(x3, ys, pos.astype(jnp.int32), min(m, 256))


STEP_HB = 8


def _positions_last(cache):
    return cache.transpose(0, 1, 3, 4, 5, 2)


def _step_head(k_t, v_t, q, k_new, v_new, slope, sink, dil):
    length = k_t.shape[1]
    pos = lax.broadcasted_iota(jnp.int32, (1, length), 1)
    s = jnp.sum(k_t * q, axis=0, keepdims=True) - slope * (length - pos).astype(F32)
    if dil > 1:
        s = jnp.where((pos & (dil - 1)) == 0, s, MASKED)
    s_new = jnp.sum(k_new * q, axis=0, keepdims=True)
    m = jnp.maximum(jnp.max(s, axis=1, keepdims=True), s_new)
    if sink is not None:
        m = jnp.maximum(m, sink)
    p = jnp.exp(s - m)
    p_new = jnp.exp(s_new - m)
    den = jnp.sum(p, axis=1, keepdims=True) + p_new
    if sink is not None:
        den = den + jnp.exp(sink - m)
    pv = jnp.sum(v_t * p, axis=1, keepdims=True) + p_new * v_new
    return pv * (1.0 / den), m + jnp.log(den)


def _step_a_kernel(q_ref, c0_ref, c1_ref, c2_ref, slope_ref, o_ref):
    for i in range(STEP_HB):
        col = slice(i, i + 1)
        slope = slope_ref[:, col]
        outs, lses = [], []
        for g, (c_ref, (_, dil)) in enumerate(zip((c0_ref, c1_ref, c2_ref), A_GROUPS)):
            o, lse = _step_head(c_ref[0, i], c_ref[1, i], q_ref[3 * g, :, col], q_ref[3 * g + 1, :, col],
                                q_ref[3 * g + 2, :, col], slope, None, dil)
            outs.append(o)
            lses.append(lse)
        top = jnp.maximum(jnp.maximum(lses[0], lses[1]), lses[2])
        es = [jnp.exp(l - top) for l in lses]
        o_ref[:, col] = ((es[0] * outs[0] + es[1] * outs[1] + es[2] * outs[2])
                         * (1.0 / (es[0] + es[1] + es[2])))


def _step_attention_a(qkv, caches, li):
    n = qkv.shape[0]
    hb = STEP_HB
    nhb = N_HEADS // hb
    n_seg = 3 * len(A_GROUPS)
    q_t = qkv.reshape(n, n_seg, nhb, hb, HEAD_DIM).transpose(0, 2, 1, 4, 3)
    slope = jnp.asarray(SLOPES, F32).reshape(nhb, 1, hb)
    cache_specs = [pl.BlockSpec((None, None, 2, hb, HEAD_DIM, c.shape[2]), lambda i, j: (li, i, 0, j, 0, 0))
                   for c in caches]
    out = pl.pallas_call(
        _step_a_kernel,
        grid=(n, nhb),
        in_specs=[pl.BlockSpec((None, None, n_seg, HEAD_DIM, hb), lambda i, j: (i, j, 0, 0, 0))] + cache_specs
        + [pl.BlockSpec((None, 1, hb), lambda i, j: (j, 0, 0))],
        out_specs=pl.BlockSpec((None, None, HEAD_DIM, hb), lambda i, j: (i, j, 0, 0)),
        out_shape=jax.ShapeDtypeStruct((n, nhb, HEAD_DIM, hb), F32),
        compiler_params=_cparams(("parallel", "parallel")),
        name="step_attn_a",
    )(q_t, *[_positions_last(c) for c in caches], slope)
    return out.transpose(0, 1, 3, 2).reshape(n, D_MODEL)


def _step_b_kernel(q_ref, kn_ref, vn_ref, c_ref, slope_ref, sink_ref, o_ref):
    rep = N_HEADS // HKV_B
    for h in range(N_HEADS):
        kv = h // rep
        o, _ = _step_head(c_ref[0, kv], c_ref[1, kv], q_ref[:, h:h + 1], kn_ref[:, kv:kv + 1],
                          vn_ref[:, kv:kv + 1], slope_ref[:, h:h + 1], sink_ref[:, h:h + 1], 1)
        o_ref[:, h:h + 1] = o


def _step_attention_b(qkv, cache, sink, li):
    n = qkv.shape[0]
    kvw = HKV_B * HEAD_DIM
    heads_last = lambda a, h: a.reshape(n, h, HEAD_DIM).transpose(0, 2, 1)
    row = lambda v: v.astype(F32).reshape(1, N_HEADS)
    out = pl.pallas_call(
        _step_b_kernel,
        grid=(n,),
        in_specs=[pl.BlockSpec((None, HEAD_DIM, N_HEADS), lambda i: (i, 0, 0)),
                  pl.BlockSpec((None, HEAD_DIM, HKV_B), lambda i: (i, 0, 0)),
                  pl.BlockSpec((None, HEAD_DIM, HKV_B), lambda i: (i, 0, 0)),
                  pl.BlockSpec((None, None, 2, HKV_B, HEAD_DIM, cache.shape[2]), lambda i: (li, i, 0, 0, 0, 0)),
                  pl.BlockSpec((1, N_HEADS), lambda i: (0, 0)), pl.BlockSpec((1, N_HEADS), lambda i: (0, 0))],
        out_specs=pl.BlockSpec((None, HEAD_DIM, N_HEADS), lambda i: (i, 0, 0)),
        out_shape=jax.ShapeDtypeStruct((n, HEAD_DIM, N_HEADS), F32),
        compiler_params=_cparams(("parallel",)),
        name="step_attn_b",
    )(heads_last(qkv[:, :D_MODEL], N_HEADS), heads_last(qkv[:, D_MODEL:D_MODEL + kvw], HKV_B),
      heads_last(qkv[:, D_MODEL + kvw:], HKV_B), _positions_last(cache), row(jnp.asarray(SLOPES, F32)), row(sink))
    return out.transpose(0, 2, 1).reshape(n, D_MODEL)


def _tile_heads(v):
    return jnp.tile(v.astype(F32), N_HEADS)


def _kept_rows(qkv, keep, k_col, width, heads):
    batch, dil, ls, _ = qkv.shape
    kv = qkv[:, :, ls - keep // dil:, k_col:k_col + 2 * width]
    kv = kv.transpose(0, 2, 1, 3).reshape(batch, keep, 2, heads, HEAD_DIM)
    return kv.astype(F32)


def kernel(x_prompt, x_sample, cache_a_w128, cache_a_w512, cache_a_w2048, cache_b, norm_mix_a, w_in_a, q_gain_a, k_gain_a, w_out_a, norm_ffn_dense, w_gu_dense, w_down_dense, norm_mix_b, w_in_b, q_gain_b, k_gain_b, sink_b, w_out_b, norm_ffn_moe, w_router, w_gu_moe, w_down_moe):
    batch, s_len, d = x_prompt.shape
    n_dec = x_sample.shape[0]
    assert x_sample.shape[1] == 1 and d == D_MODEL
    caches_a = (cache_a_w128, cache_a_w512, cache_a_w2048)
    xp = x_prompt.reshape(batch * s_len, d)
    xs = x_sample.reshape(n_dec, d)
    q_scale = HEAD_DIM ** -0.5
    n_grp = len(A_GROUPS)
    outs = []

    li = 0
    w_in = w_in_a[li].astype(BF16)
    cols_a = w_in.shape[1]
    gain = jnp.concatenate([jnp.concatenate([_tile_heads(q_gain_a[li, g]) * q_scale, _tile_heads(k_gain_a[li, g]),
                                             jnp.ones((d,), F32)]) for g in range(n_grp)]).reshape(1, cols_a)
    flag = jnp.tile(jnp.concatenate([jnp.ones((2 * d,), F32), jnp.zeros((d,), F32)]), n_grp).reshape(1, cols_a)
    qkv_s = _proj(xs, norm_mix_a[li], w_in, gain, flag, batch=1, dil=1, tn=d, col0=0, ncols=cols_a)
    qkv_s = qkv_s.reshape(n_dec, cols_a)
    no_sink = jnp.zeros((N_HEADS,), F32)
    o_p, l_p = [], []
    for g, (window, dil) in enumerate(A_GROUPS):
        qkv_g = _proj(xp, norm_mix_a[li], w_in, gain, flag, batch=batch, dil=dil, tn=d, col0=3 * g, ncols=3 * d)
        o, l = _band_attention(qkv_g, no_sink, q_blk=0, k_blk=1, v_blk=2, kv_heads=N_HEADS, has_sink=False)
        o_p.append(o)
        l_p.append(l)
        outs.append(_kept_rows(qkv_g, min(window, s_len), d, d, N_HEADS)[None])
        kv_s = qkv_s[:, (3 * g + 1) * d:(3 * g + 3) * d]
        outs.append(kv_s.reshape(1, n_dec, 1, 2, N_HEADS, HEAD_DIM).astype(F32))
    w_out = w_out_a[li].astype(BF16)
    xp = _merge_out(o_p, l_p, w_out, xp)
    xs = _out_proj(_step_attention_a(qkv_s.astype(F32), caches_a, li), w_out, xs, tiled=False)
    wgu, wd = w_gu_dense[li].astype(BF16), w_down_dense[li].astype(BF16)
    xp = _ffn_dense(xp, norm_ffn_dense[li], wgu, wd)
    xs = _ffn_dense(xs, norm_ffn_dense[li], wgu, wd)

    w_in = w_in_b[li].astype(BF16)
    cols_b = w_in.shape[1]
    kvw = HKV_B * HEAD_DIM
    gain = jnp.concatenate([_tile_heads(q_gain_b[li]) * q_scale, jnp.tile(k_gain_b[li].astype(F32), HKV_B),
                            jnp.ones((kvw,), F32)]).reshape(1, cols_b)
    flag = jnp.concatenate([jnp.ones((d + kvw,), F32), jnp.zeros((kvw,), F32)]).reshape(1, cols_b)
    qkv_p = _proj(xp, norm_mix_b[li], w_in, gain, flag, batch=batch, dil=1, tn=cols_b, col0=0, ncols=cols_b)
    qkv_s = _proj(xs, norm_mix_b[li], w_in, gain, flag, batch=1, dil=1, tn=cols_b, col0=0, ncols=cols_b)
    qkv_s = qkv_s.reshape(n_dec, cols_b).astype(F32)
    sink = sink_b[li].astype(F32)
    o, _ = _band_attention(qkv_p, sink, q_blk=0, k_blk=d // kvw, v_blk=d // kvw + 1, kv_heads=HKV_B, has_sink=True)
    outs.append(_kept_rows(qkv_p, min(N_STEPS, s_len), d, kvw, HKV_B)[None])
    outs.append(qkv_s[:, d:].reshape(1, n_dec, 1, 2, HKV_B, HEAD_DIM))
    w_out = w_out_b[li].astype(BF16)
    xp3 = _out_proj(o.reshape(batch * s_len, d), w_out, xp, tiled=True)
    xs3 = _out_proj(_step_attention_b(qkv_s, cache_b, sink, li), w_out, xs, tiled=True)
    wgu, wd = w_gu_moe[li].astype(BF16), w_down_moe[li].astype(BF16)
    xp = _moe(xp3, norm_ffn_moe[li], w_router[li], wgu, wd, 512)
    xs = _moe(xs3, norm_ffn_moe[li], w_router[li], wgu, wd, 128)

    return (xp.reshape(batch, s_len, d), xs.reshape(n_dec, 1, d), *outs)
```

```python
import functools

import numpy as np
import jax
import jax.numpy as jnp
from jax import lax
from jax.experimental import pallas as pl
from jax.experimental.pallas import tpu as pltpu

F32 = jnp.float32
BF16 = jnp.bfloat16

D_MODEL = 1024
HEAD_DIM = 64
N_HEADS = 16
HKV_B = 2
A_GROUPS = ((128, 1), (512, 4), (2048, 16))
N_STEPS = 128
BLK = 128
N_EXPERTS = 8
RMS_EPS = 1e-6
MASKED = -1e30
LANES = 128
SUBLANES = 8
MXU_DIM = 256
VMEM_LIMIT = 56 * 1024 * 1024
D_TILES = D_MODEL // LANES

SLOPES = tuple(float(2.0 ** (-8.0 * (h + 1) / N_HEADS)) for h in range(N_HEADS))


def _cparams(sem):
    return pltpu.CompilerParams(dimension_semantics=sem, vmem_limit_bytes=VMEM_LIMIT)


def _rms(x, g):
    ms = jnp.mean(x * x, axis=-1, keepdims=True)
    return x * lax.rsqrt(ms + RMS_EPS) * g


def _proj_kernel(x_ref, g_ref, w_ref, gain_ref, flag_ref, seg_ref, o_ref, h_scr, *res_scr, dil):
    @pl.when(pl.program_id(1) == 0)
    def _():
        h_scr[...] = _rms(x_ref[...], g_ref[...]).astype(BF16)

    acc = jnp.dot(h_scr[...], w_ref[...], preferred_element_type=F32)
    sq = (acc * acc).astype(BF16)
    tm, tn = acc.shape
    ssq = jnp.concatenate(
        [jnp.dot(sq[:, c * MXU_DIM:(c + 1) * MXU_DIM], seg_ref[...], preferred_element_type=F32)
         for c in range(tn // MXU_DIM)], axis=1)
    nrm = acc * lax.rsqrt(ssq * (1.0 / HEAD_DIM) + RMS_EPS) * gain_ref[...]
    res = jnp.where(flag_ref[...] > 0.0, nrm, acc)
    if dil == 1:
        o_ref[0, 0] = res.astype(o_ref.dtype)
    else:
        scr = res_scr[0]
        for c in range(tn // LANES):
            cols = slice(c * LANES, (c + 1) * LANES)
            scr[c] = res[:, cols]
            for r in range(dil):
                o_ref[0, r, :, cols] = scr[c, pl.ds(r, tm // dil, stride=dil), :].astype(o_ref.dtype)


def _proj(x, g, w, gain, flag, *, batch, dil, tn, col0, ncols):
    m, d = x.shape
    s_len = m // batch
    tm = min(s_len, 1024)
    tpb = s_len // tm
    seg = jnp.asarray(np.kron(np.eye(MXU_DIM // HEAD_DIM), np.ones((HEAD_DIM, HEAD_DIM))), BF16)
    scratch = [pltpu.VMEM((tm, d), BF16)] + ([pltpu.VMEM((tn // LANES, tm, LANES), F32)] if dil > 1 else [])
    return pl.pallas_call(
        functools.partial(_proj_kernel, dil=dil),
        grid=(m // tm, ncols // tn),
        in_specs=[
            pl.BlockSpec((tm, d), lambda i, j: (i, 0)),
            pl.BlockSpec((1, d), lambda i, j: (0, 0)),
            pl.BlockSpec((d, tn), lambda i, j: (0, col0 + j)),
            pl.BlockSpec((1, tn), lambda i, j: (0, col0 + j)),
            pl.BlockSpec((1, tn), lambda i, j: (0, col0 + j)),
            pl.BlockSpec((MXU_DIM, MXU_DIM), lambda i, j: (0, 0)),
        ],
        out_specs=pl.BlockSpec((1, dil, tm // dil, tn), lambda i, j: (i // tpb, 0, i % tpb, j)),
        out_shape=jax.ShapeDtypeStruct((batch, dil, s_len // dil, ncols), BF16),
        scratch_shapes=scratch,
        compiler_params=_cparams(("parallel", "arbitrary")),
        name="proj_qknorm_d%d" % dil,
    )(x, g.reshape(1, d), w, gain, flag, seg)


def _band_kernel(sink_ref, q_ref, kp_ref, kc_ref, vp_ref, vc_ref, o_ref, lse_ref, k_all, v_all, *, dil, qb,
                 kv_heads, has_sink):
    c = pl.program_id(2)
    k_all[0:BLK] = kp_ref[0, 0]
    k_all[BLK:] = kc_ref[0, 0]
    v_all[0:BLK] = vp_ref[0, 0]
    v_all[BLK:] = vc_ref[0, 0]

    qi = lax.broadcasted_iota(jnp.int32, (BLK, 2 * BLK), 0)
    sj = lax.broadcasted_iota(jnp.int32, (BLK, 2 * BLK), 1)
    dist = qi - sj + BLK
    base = jnp.where((dist >= 0) & (dist <= N_STEPS), -(dist.astype(F32) * float(dil)), MASKED)
    base_first = jnp.where(sj >= BLK, base, MASKED)
    low_k = lax.broadcasted_iota(jnp.int32, (2 * BLK, LANES), 1) < HEAD_DIM
    low_q = lax.broadcasted_iota(jnp.int32, (BLK, LANES), 1) < HEAD_DIM
    zeros_k = jnp.zeros((2 * BLK, LANES), BF16)
    ones_a = jnp.where(low_k, 1.0, 0.0).astype(BF16)
    ones_b = jnp.where(low_k, 0.0, 1.0).astype(BF16)
    nt = (((1,), (1,)), ((), ()))

    for i in range(qb):
        rows = slice(i * BLK, (i + 1) * BLK)
        win = slice(i * BLK, (i + 2) * BLK)
        base_i = jnp.where(c == 0, base_first, base) if i == 0 else base
        if kv_heads != N_HEADS:
            kw, vw = k_all[win, :], v_all[win, :]
            zero_half = jnp.zeros((2 * BLK, HEAD_DIM), BF16)
            shared = {}
            for kv in range(kv_heads):
                k_kv = kw[:, kv * HEAD_DIM:(kv + 1) * HEAD_DIM]
                v_kv = vw[:, kv * HEAD_DIM:(kv + 1) * HEAD_DIM]
                shared[kv] = tuple((jnp.concatenate([t, zero_half], axis=1), jnp.concatenate([zero_half, t], axis=1))
                                   for t in (k_kv, v_kv))
        for pair in range(N_HEADS // 2):
            pc = slice(pair * LANES, (pair + 1) * LANES)
            ha, hb = 2 * pair, 2 * pair + 1
            qp = q_ref[0, 0, rows, pc]
            if kv_heads == N_HEADS:
                kw, vw = k_all[win, pc], v_all[win, pc]
                k_a, k_b = jnp.where(low_k, kw, zeros_k), jnp.where(low_k, zeros_k, kw)
                v_a, v_b = jnp.where(low_k, vw, zeros_k), jnp.where(low_k, zeros_k, vw)
            else:
                (k_a, k_b), (v_a, v_b) = shared[ha // (N_HEADS // kv_heads)]
            s_a = lax.dot_general(qp, k_a, nt, preferred_element_type=F32) + SLOPES[ha] * base_i
            s_b = lax.dot_general(qp, k_b, nt, preferred_element_type=F32) + SLOPES[hb] * base_i
            m_a = jnp.max(s_a, axis=-1, keepdims=True)
            m_b = jnp.max(s_b, axis=-1, keepdims=True)
            if has_sink:
                m_a = jnp.maximum(m_a, sink_ref[ha])
                m_b = jnp.maximum(m_b, sink_ref[hb])
            p_a = jnp.exp(s_a - m_a).astype(BF16)
            p_b = jnp.exp(s_b - m_b).astype(BF16)
            res = (jnp.dot(p_a, jnp.concatenate([v_a, ones_a], axis=1), preferred_element_type=F32)
                   + jnp.dot(p_b, jnp.concatenate([v_b, ones_b], axis=1), preferred_element_type=F32))
            den = res[:, LANES:]
            m_pair = jnp.where(low_q, m_a, m_b)
            if has_sink:
                den = den + jnp.where(low_q, jnp.exp(sink_ref[ha] - m_a), jnp.exp(sink_ref[hb] - m_b))
            o_ref[0, 0, rows, pc] = (res[:, :LANES] * (1.0 / den)).astype(o_ref.dtype)
            lse_ref[0, 0, rows, pc] = m_pair + jnp.log(den)


def _band_attention(qkv, sink, *, q_blk, k_blk, v_blk, kv_heads, has_sink):
    batch, dil, ls, _ = qkv.shape
    qb = min(4, ls // BLK)
    rows = BLK * qb
    kvw = kv_heads * HEAD_DIM
    kern = functools.partial(_band_kernel, dil=dil, qb=qb, kv_heads=kv_heads, has_sink=has_sink)
    prev = lambda b, r, c: jnp.maximum(c * qb - 1, 0)
    return pl.pallas_call(
        kern,
        grid=(batch, dil, ls // rows),
        in_specs=[
            pl.BlockSpec(memory_space=pltpu.SMEM),
            pl.BlockSpec((1, 1, rows, D_MODEL), lambda b, r, c: (b, r, c, q_blk)),
            pl.BlockSpec((1, 1, BLK, kvw), lambda b, r, c: (b, r, prev(b, r, c), k_blk)),
            pl.BlockSpec((1, 1, rows, kvw), lambda b, r, c: (b, r, c, k_blk)),
            pl.BlockSpec((1, 1, BLK, kvw), lambda b, r, c: (b, r, prev(b, r, c), v_blk)),
            pl.BlockSpec((1, 1, rows, kvw), lambda b, r, c: (b, r, c, v_blk)),
        ],
        out_specs=[pl.BlockSpec((1, 1, rows, D_MODEL), lambda b, r, c: (b, r, c, 0))] * 2,
        out_shape=[jax.ShapeDtypeStruct((batch, dil, ls, D_MODEL), BF16),
                   jax.ShapeDtypeStruct((batch, dil, ls, D_MODEL), F32)],
        scratch_shapes=[pltpu.VMEM((BLK + rows, kvw), BF16), pltpu.VMEM((BLK + rows, kvw), BF16)],
        compiler_params=_cparams(("parallel", "parallel", "arbitrary")),
        name="band_attn_d%d" % dil,
    )(sink, qkv, qkv, qkv, qkv, qkv)


def _merge_out_kernel(*refs, dils):
    o_refs, l_refs = refs[0:3], refs[3:6]
    w_ref, x_ref, out_ref = refs[6:9]
    scr = refs[9:]
    tm = x_ref.shape[0]
    outs, lses = [], []
    k = 0
    for o_ref, l_ref, dil in zip(o_refs, l_refs, dils):
        if dil == 1:
            outs.append(o_ref[0, 0].astype(F32))
            lses.append(l_ref[0, 0])
        else:
            so, sl = scr[k], scr[k + 1]
            k += 2
            for c in range(D_TILES):
                cols = slice(c * LANES, (c + 1) * LANES)
                for r in range(dil):
                    so[c, pl.ds(r, tm // dil, stride=dil), :] = o_ref[0, r, :, cols].astype(F32)
                    sl[c, pl.ds(r, tm // dil, stride=dil), :] = l_ref[0, r, :, cols]
            outs.append(jnp.concatenate([so[c] for c in range(D_TILES)], axis=1))
            lses.append(jnp.concatenate([sl[c] for c in range(D_TILES)], axis=1))
    top = jnp.maximum(jnp.maximum(lses[0], lses[1]), lses[2])
    es = [jnp.exp(l - top) for l in lses]
    mix = (es[0] * outs[0] + es[1] * outs[1] + es[2] * outs[2]) * (1.0 / (es[0] + es[1] + es[2]))
    out_ref[...] = x_ref[...] + jnp.dot(mix.astype(BF16), w_ref[...], preferred_element_type=F32)


def _merge_out(outs, lses, w, x):
    m, d = x.shape
    batch = outs[0].shape[0]
    s_len = m // batch
    tm = min(s_len, 256)
    tpb = s_len // tm
    dils = tuple(o.shape[1] for o in outs)
    grp = [pl.BlockSpec((1, dil, tm // dil, d), lambda i: (i // tpb, 0, i % tpb, 0)) for dil in dils]
    scratch = [pltpu.VMEM((D_TILES, tm, LANES), F32) for dil in dils if dil > 1 for _ in range(2)]
    return pl.pallas_call(
        functools.partial(_merge_out_kernel, dils=dils),
        grid=(m // tm,),
        in_specs=grp + grp + [pl.BlockSpec((d, d), lambda i: (0, 0)), pl.BlockSpec((tm, d), lambda i: (i, 0))],
        out_specs=pl.BlockSpec((tm, d), lambda i: (i, 0)),
        out_shape=jax.ShapeDtypeStruct((m, d), F32),
        scratch_shapes=scratch,
        compiler_params=_cparams(("parallel",)),
        name="merge_outproj",
    )(*outs, *lses, w, x)


def _out_kernel(o_ref, w_ref, x_ref, out_ref):
    out_ref[...] = x_ref[...] + jnp.dot(o_ref[...].astype(BF16), w_ref[...], preferred_element_type=F32)


def _out_proj(o, w, x):
    m, d = x.shape
    tm = min(m, 1024)
    return pl.pallas_call(
        _out_kernel,
        grid=(m // tm,),
        in_specs=[pl.BlockSpec((tm, d), lambda i: (i, 0)), pl.BlockSpec((d, d), lambda i: (0, 0)),
                  pl.BlockSpec((tm, d), lambda i: (i, 0))],
        out_specs=pl.BlockSpec((tm, d), lambda i: (i, 0)),
        out_shape=jax.ShapeDtypeStruct((m, d), F32),
        compiler_params=_cparams(("parallel",)),
        name="outproj",
    )(o, w, x)


def _silu(g):
    return g * (1.0 / (1.0 + jnp.exp(-g)))


def _ffn_kernel(x_ref, g_ref, wgu_ref, wd_ref, out_ref, *, d_ff, chunk):
    x = x_ref[...]
    h = _rms(x, g_ref[...]).astype(BF16)
    acc = jnp.zeros(x.shape, F32)
    for c in range(d_ff // chunk):
        gate = jnp.dot(h, wgu_ref[:, c * chunk:(c + 1) * chunk], preferred_element_type=F32)
        up = jnp.dot(h, wgu_ref[:, d_ff + c * chunk:d_ff + (c + 1) * chunk], preferred_element_type=F32)
        act = (_silu(gate) * up).astype(BF16)
        acc = acc + jnp.dot(act, wd_ref[c * chunk:(c + 1) * chunk, :], preferred_element_type=F32)
    out_ref[...] = x + acc


def _ffn_dense(x, g, wgu, wd):
    m, d = x.shape
    tm = min(m, 512)
    d_ff = wd.shape[0]
    kern = functools.partial(_ffn_kernel, d_ff=d_ff, chunk=d_ff // 2)
    return pl.pallas_call(
        kern,
        grid=(m // tm,),
        in_specs=[pl.BlockSpec((tm, d), lambda i: (i, 0)), pl.BlockSpec((1, d), lambda i: (0, 0)),
                  pl.BlockSpec((d, 2 * d_ff), lambda i: (0, 0), pipeline_mode=pl.Buffered(1)),
                  pl.BlockSpec((d_ff, d), lambda i: (0, 0), pipeline_mode=pl.Buffered(1))],
        out_specs=pl.BlockSpec((tm, d), lambda i: (i, 0)),
        out_shape=jax.ShapeDtypeStruct((m, d), F32),
        compiler_params=_cparams(("parallel",)),
        name="ffn_dense",
    )(x, g.reshape(1, d), wgu, wd)


def _router_kernel(x_ref, g_ref, wr_ref, idx_ref, gate_ref):
    h = _rms(x_ref[...], g_ref[...])
    logits = jnp.dot(h, wr_ref[...], preferred_element_type=F32, precision=lax.Precision.HIGHEST)
    lane = lax.broadcasted_iota(jnp.int32, logits.shape, 1)
    logits = jnp.where(lane < N_EXPERTS, logits, -jnp.inf)
    t1 = jnp.max(logits, axis=-1, keepdims=True)
    i1 = jnp.min(jnp.where(logits == t1, lane, LANES), axis=-1, keepdims=True)
    rest = jnp.where(lane == i1, -jnp.inf, logits)
    t2 = jnp.max(rest, axis=-1, keepdims=True)
    i2 = jnp.min(jnp.where(rest == t2, lane, LANES), axis=-1, keepdims=True)
    e2 = jnp.exp(t2 - t1)
    g1 = 1.0 / (1.0 + e2)
    g2 = e2 / (1.0 + e2)
    idx_ref[...] = jnp.where(lane == 0, i1, jnp.where(lane == 1, i2, 0))
    gate_ref[...] = jnp.where(lane == 0, g1, jnp.where(lane == 1, g2, 0.0))


def _router(x, g, w_router):
    m, d = x.shape
    tm = min(m, 1024)
    wr = jnp.zeros((d, LANES), F32).at[:, :N_EXPERTS].set(w_router)
    return pl.pallas_call(
        _router_kernel,
        grid=(m // tm,),
        in_specs=[pl.BlockSpec((tm, d), lambda i: (i, 0)), pl.BlockSpec((1, d), lambda i: (0, 0)),
                  pl.BlockSpec((d, LANES), lambda i: (0, 0))],
        out_specs=[pl.BlockSpec((tm, LANES), lambda i: (i, 0))] * 2,
        out_shape=[jax.ShapeDtypeStruct((m, LANES), jnp.int32), jax.ShapeDtypeStruct((m, LANES), F32)],
        compiler_params=_cparams(("parallel",)),
        name="router_top2",
    )(x, g.reshape(1, d), wr)


ROW_UNROLL = 4


def _row_copy(src, dst, sem, src_row, dst_row):
    return pltpu.make_async_copy(src.at[pl.ds(src_row, 1)], dst.at[pl.ds(dst_row, 1)], sem)


ZERO_ROWS = 128


def _dispatch_kernel(pad_ref, pos_ref, x_ref, xs_hbm, zeros, sem, *, rows):
    @pl.when(pl.program_id(0) == 0)
    def _():
        zeros[...] = jnp.zeros(zeros.shape, zeros.dtype)

        def pad_start(r, carry):
            _row_copy(zeros, xs_hbm, sem, 0, r).start()
            return carry

        def pad_wait(r, carry):
            _row_copy(zeros, xs_hbm, sem, 0, r).wait()
            return carry

        for e in range(N_EXPERTS):
            lax.fori_loop(pad_ref[e], pad_ref[N_EXPERTS + e], pad_start, 0)
            lax.fori_loop(pad_ref[e], pad_ref[N_EXPERTS + e], pad_wait, 0)

        def tail_copy(b):
            return pltpu.make_async_copy(zeros, xs_hbm.at[pl.ds(b * ZERO_ROWS, ZERO_ROWS)], sem)

        def tail_start(b, carry):
            tail_copy(b).start()
            return carry

        def tail_wait(b, carry):
            tail_copy(b).wait()
            return carry

        first, last = pad_ref[2 * N_EXPERTS - 1] // ZERO_ROWS, xs_hbm.shape[0] // ZERO_ROWS
        lax.fori_loop(first, last, tail_start, 0)
        lax.fori_loop(first, last, tail_wait, 0)

    def start(t, carry):
        for u in range(ROW_UNROLL):
            r = t * ROW_UNROLL + u
            _row_copy(x_ref, xs_hbm, sem, r, pos_ref[0, 0, 2 * r]).start(priority=0)
            _row_copy(x_ref, xs_hbm, sem, r, pos_ref[0, 0, 2 * r + 1]).start(priority=1)
        return carry

    def wait(t, carry):
        for u in range(2 * ROW_UNROLL):
            _row_copy(x_ref, xs_hbm, sem, 0, 0).wait()
        return carry

    lax.fori_loop(0, rows // ROW_UNROLL, start, 0)
    lax.fori_loop(0, rows // ROW_UNROLL, wait, 0)


def _dispatch(x, pos, pad, n_rows, rows):
    m, d = x.shape
    assert n_rows % ZERO_ROWS == 0
    return pl.pallas_call(
        functools.partial(_dispatch_kernel, rows=rows),
        grid_spec=pltpu.PrefetchScalarGridSpec(
            num_scalar_prefetch=1,
            grid=(m // rows,),
            in_specs=[pl.BlockSpec((1, 1, 2 * rows), lambda i, pad_ref: (i, 0, 0), memory_space=pltpu.SMEM),
                      pl.BlockSpec((rows, d), lambda i, pad_ref: (i, 0))],
            out_specs=pl.BlockSpec(memory_space=pl.ANY),
            scratch_shapes=[pltpu.VMEM((ZERO_ROWS, d), F32), pltpu.SemaphoreType.DMA(())],
        ),
        out_shape=jax.ShapeDtypeStruct((n_rows, d), x.dtype),
        compiler_params=_cparams(("arbitrary",)),
        name="moe_dispatch",
    )(pad, pos.reshape(m // rows, 1, 2 * rows), x)


def _expert_kernel(te_ref, nv_ref, xs_ref, g_ref, wgu_ref, wd_ref, o_ref, *, d_ff):
    @pl.when(pl.program_id(0) < nv_ref[0])
    def _():
        h = _rms(xs_ref[...], g_ref[...]).astype(BF16)
        gate = jnp.dot(h, wgu_ref[0, :, :d_ff], preferred_element_type=F32)
        up = jnp.dot(h, wgu_ref[0, :, d_ff:], preferred_element_type=F32)
        act = (_silu(gate) * up).astype(BF16)
        o_ref[...] = jnp.dot(act, wd_ref[0], preferred_element_type=F32)

    @pl.when(pl.program_id(0) >= nv_ref[0])
    def _():
        o_ref[...] = jnp.zeros(o_ref.shape, o_ref.dtype)


def _experts(xs, g, tile_expert, n_valid, wgu, wd, tm):
    n, d = xs.shape
    d_ff = wd.shape[1]
    row_tile = lambda i, te, nv: (jnp.minimum(i, nv[0] - 1), 0)
    return pl.pallas_call(
        functools.partial(_expert_kernel, d_ff=d_ff),
        grid_spec=pltpu.PrefetchScalarGridSpec(
            num_scalar_prefetch=2,
            grid=(n // tm,),
            in_specs=[
                pl.BlockSpec((tm, d), row_tile),
                pl.BlockSpec((1, d), lambda i, te, nv: (0, 0)),
                pl.BlockSpec((1, d, 2 * d_ff), lambda i, te, nv: (te[i], 0, 0)),
                pl.BlockSpec((1, d_ff, d), lambda i, te, nv: (te[i], 0, 0)),
            ],
            out_specs=pl.BlockSpec((tm, d), lambda i, te, nv: (i, 0)),
        ),
        out_shape=jax.ShapeDtypeStruct((n, d), F32),
        compiler_params=_cparams(("arbitrary",)),
        name="moe_experts",
    )(tile_expert, n_valid, xs, g.reshape(1, d), wgu, wd)


def _combine_kernel(pos_ref, x_ref, gate_ref, y_hbm, o_ref, buf0, buf1, sem, *, rows):
    def start(t, carry):
        for u in range(ROW_UNROLL):
            r = t * ROW_UNROLL + u
            _row_copy(y_hbm, buf0, sem.at[0], pos_ref[0, 0, 2 * r], r).start(priority=0)
            _row_copy(y_hbm, buf1, sem.at[1], pos_ref[0, 0, 2 * r + 1], r).start(priority=1)
        return carry

    def wait(t, carry):
        for u in range(ROW_UNROLL):
            r = t * ROW_UNROLL + u
            _row_copy(y_hbm, buf0, sem.at[0], 0, r).wait()
            _row_copy(y_hbm, buf1, sem.at[1], 0, r).wait()
        return carry

    lax.fori_loop(0, rows // ROW_UNROLL, start, 0)
    lax.fori_loop(0, rows // ROW_UNROLL, wait, 0)
    o_ref[...] = x_ref[...] + (gate_ref[:, 0:1] * buf0[...] + gate_ref[:, 1:2] * buf1[...])


def _combine(x, ys, pos, gate, rows):
    m, d = x.shape
    return pl.pallas_call(
        functools.partial(_combine_kernel, rows=rows),
        grid=(m // rows,),
        in_specs=[pl.BlockSpec((1, 1, 2 * rows), lambda i: (i, 0, 0), memory_space=pltpu.SMEM),
                  pl.BlockSpec((rows, d), lambda i: (i, 0)), pl.BlockSpec((rows, LANES), lambda i: (i, 0)),
                  pl.BlockSpec(memory_space=pl.ANY)],
        out_specs=pl.BlockSpec((rows, d), lambda i: (i, 0)),
        scratch_shapes=[pltpu.VMEM((rows, d), F32), pltpu.VMEM((rows, d), F32), pltpu.SemaphoreType.DMA((2,))],
        out_shape=jax.ShapeDtypeStruct((m, d), F32),
        compiler_params=_cparams(("arbitrary",)),
        name="moe_combine",
    )(pos.reshape(m // rows, 1, 2 * rows), x, gate, ys)


def _moe(x, g, w_router, wgu, wd, tm):
    m = x.shape[0]
    idx, gate = _router(x, g, w_router)
    e_flat = idx[:, :2].reshape(-1)
    onehot = (e_flat[:, None] == jnp.arange(N_EXPERTS)[None, :]).astype(jnp.int32)
    counts = jnp.sum(onehot, axis=0)
    rank = jnp.sum((jnp.cumsum(onehot, axis=0) - onehot) * onehot, axis=1)
    padded = ((counts + tm - 1) // tm) * tm
    ends = jnp.cumsum(padded)
    starts = ends - padded
    pos = (jnp.sum(onehot * starts[None, :], axis=1) + rank).astype(jnp.int32)
    pad = jnp.concatenate([starts + counts, ends]).astype(jnp.int32)
    n_tiles = -(-(2 * m + N_EXPERTS * (tm - 1)) // tm)
    tile_start = jnp.arange(n_tiles, dtype=jnp.int32) * tm
    tile_expert = jnp.minimum(jnp.sum((tile_start[:, None] >= ends[None, :]).astype(jnp.int32), axis=1),
                              N_EXPERTS - 1).astype(jnp.int32)
    n_valid = (ends[-1] // tm).astype(jnp.int32).reshape(1)

    rows = min(m, 256)
    xs = _dispatch(x, pos, pad, n_tiles * tm, rows)
    ys = _experts(xs, g, tile_expert, n_valid, wgu, wd, tm)
    return _combine(x, ys, pos, gate, rows)


STEP_HB = 8


def _positions_last(cache):
    return cache.transpose(0, 1, 3, 4, 5, 2)


def _step_softmax(s, s_new, slope, sink, dil):
    length = s.shape[1]
    pos = lax.broadcasted_iota(jnp.int32, (1, length), 1)
    s = s - slope * (length - pos).astype(F32)
    if dil > 1:
        s = jnp.where((pos & (dil - 1)) == 0, s, MASKED)
    m = jnp.maximum(jnp.max(s, axis=1, keepdims=True), s_new)
    if sink is not None:
        m = jnp.maximum(m, sink)
    p = jnp.exp(s - m)
    p_new = jnp.exp(s_new - m)
    den = jnp.sum(p, axis=1, keepdims=True) + p_new
    if sink is not None:
        den = den + jnp.exp(sink - m)
    return p, p_new, 1.0 / den, m + jnp.log(den)


def _step_a_kernel(q_ref, c0_ref, c1_ref, c2_ref, slope_ref, o_ref):
    heads = range(STEP_HB)
    slope = slope_ref[...]
    outs, lses = [], []
    for g, (c_ref, (_, dil)) in enumerate(zip((c0_ref, c1_ref, c2_ref), A_GROUPS)):
        q = [q_ref[3 * g, :, i:i + 1] for i in heads]
        s = jnp.concatenate([jnp.sum(c_ref[0, i] * q[i], axis=0, keepdims=True) for i in heads], axis=0)
        s_new = jnp.concatenate([jnp.sum(q_ref[3 * g + 1, :, i:i + 1] * q[i], axis=0, keepdims=True)
                                 for i in heads], axis=0)
        p, p_new, inv, lse = _step_softmax(s, s_new, slope, None, dil)
        outs.append([(jnp.sum(c_ref[1, i] * p[i:i + 1, :], axis=1, keepdims=True)
                      + p_new[i:i + 1, :] * q_ref[3 * g + 2, :, i:i + 1]) * inv[i:i + 1, :] for i in heads])
        lses.append(lse)
    top = jnp.maximum(jnp.maximum(lses[0], lses[1]), lses[2])
    es = [jnp.exp(l - top) for l in lses]
    inv = 1.0 / (es[0] + es[1] + es[2])
    ws = [e * inv for e in es]
    o_ref[...] = jnp.concatenate(
        [sum(ws[g][i:i + 1, :] * outs[g][i] for g in range(len(A_GROUPS))) for i in heads], axis=1)


def _step_attention_a(qkv, caches, li):
    n = qkv.shape[0]
    hb = STEP_HB
    nhb = N_HEADS // hb
    n_seg = 3 * len(A_GROUPS)
    q_t = qkv.reshape(n, n_seg, nhb, hb, HEAD_DIM).transpose(0, 2, 1, 4, 3)
    slope = jnp.asarray(SLOPES, F32).reshape(nhb, hb, 1)
    cache_specs = [pl.BlockSpec((None, None, 2, hb, HEAD_DIM, c.shape[2]), lambda i, j: (li, i, 0, j, 0, 0))
                   for c in caches]
    out = pl.pallas_call(
        _step_a_kernel,
        grid=(n, nhb),
        in_specs=[pl.BlockSpec((None, None, n_seg, HEAD_DIM, hb), lambda i, j: (i, j, 0, 0, 0))] + cache_specs
        + [pl.BlockSpec((None, hb, 1), lambda i, j: (j, 0, 0))],
        out_specs=pl.BlockSpec((None, None, HEAD_DIM, hb), lambda i, j: (i, j, 0, 0)),
        out_shape=jax.ShapeDtypeStruct((n, nhb, HEAD_DIM, hb), F32),
        compiler_params=_cparams(("parallel", "parallel")),
        name="step_attn_a",
    )(q_t, *[_positions_last(c) for c in caches], slope)
    return out.transpose(0, 1, 3, 2).reshape(n, D_MODEL)


STEP_TB = 8


def _step_b_kernel(q_ref, kn_ref, vn_ref, c_ref, slope_ref, sink_ref, o_ref):
    rep = N_HEADS // HKV_B
    hi = lax.Precision.HIGHEST
    for t in range(q_ref.shape[0]):
        for kv in range(HKV_B):
            hs = slice(kv * rep, (kv + 1) * rep)
            q = q_ref[t, hs, :]
            s = jnp.dot(q, c_ref[t, 0, kv], preferred_element_type=F32, precision=hi)
            s_new = jnp.sum(q * kn_ref[t, kv:kv + 1, :], axis=1, keepdims=True)
            p, p_new, inv, _ = _step_softmax(s, s_new, slope_ref[hs, :], sink_ref[hs, :], 1)
            pv = lax.dot_general(p, c_ref[t, 1, kv], (((1,), (1,)), ((), ())), preferred_element_type=F32,
                                 precision=hi)
            o_ref[t, hs, :] = (pv + p_new * vn_ref[t, kv:kv + 1, :]) * inv


def _step_attention_b(qkv, cache, sink, li):
    n = qkv.shape[0]
    kvw = HKV_B * HEAD_DIM
    tb = min(n, STEP_TB)
    col = lambda v: v.astype(F32).reshape(N_HEADS, 1)
    tok = lambda h: pl.BlockSpec((tb, h, HEAD_DIM), lambda i: (i, 0, 0))
    out = pl.pallas_call(
        _step_b_kernel,
        grid=(n // tb,),
        in_specs=[tok(N_HEADS), tok(HKV_B), tok(HKV_B),
                  pl.BlockSpec((None, tb, 2, HKV_B, HEAD_DIM, cache.shape[2]), lambda i: (li, i, 0, 0, 0, 0)),
                  pl.BlockSpec((N_HEADS, 1), lambda i: (0, 0)), pl.BlockSpec((N_HEADS, 1), lambda i: (0, 0))],
        out_specs=tok(N_HEADS),
        out_shape=jax.ShapeDtypeStruct((n, N_HEADS, HEAD_DIM), F32),
        compiler_params=_cparams(("parallel",)),
        name="step_attn_b",
    )(qkv[:, :D_MODEL].reshape(n, N_HEADS, HEAD_DIM), qkv[:, D_MODEL:D_MODEL + kvw].reshape(n, HKV_B, HEAD_DIM),
      qkv[:, D_MODEL + kvw:].reshape(n, HKV_B, HEAD_DIM), _positions_last(cache), col(jnp.asarray(SLOPES, F32)),
      col(sink))
    return out.reshape(n, D_MODEL)


def _tile_heads(v):
    return jnp.tile(v.astype(F32), N_HEADS)


def _kept_rows(qkv, keep, k_col, width, heads):
    batch, dil, ls, _ = qkv.shape
    kv = qkv[:, :, ls - keep // dil:, k_col:k_col + 2 * width]
    kv = kv.transpose(0, 2, 1, 3).reshape(batch, keep, 2, heads, HEAD_DIM)
    return kv.astype(F32)


def kernel(x_prompt, x_sample, cache_a_w128, cache_a_w512, cache_a_w2048, cache_b, norm_mix_a, w_in_a, q_gain_a, k_gain_a, w_out_a, norm_ffn_dense, w_gu_dense, w_down_dense, norm_mix_b, w_in_b, q_gain_b, k_gain_b, sink_b, w_out_b, norm_ffn_moe, w_router, w_gu_moe, w_down_moe):
    batch, s_len, d = x_prompt.shape
    n_dec = x_sample.shape[0]
    assert x_sample.shape[1] == 1 and d == D_MODEL
    caches_a = (cache_a_w128, cache_a_w512, cache_a_w2048)
    xp = x_prompt.reshape(batch * s_len, d)
    xs = x_sample.reshape(n_dec, d)
    q_scale = HEAD_DIM ** -0.5
    n_grp = len(A_GROUPS)
    outs = []

    li = 0
    w_in = w_in_a[li].astype(BF16)
    cols_a = w_in.shape[1]
    gain = jnp.concatenate([jnp.concatenate([_tile_heads(q_gain_a[li, g]) * q_scale, _tile_heads(k_gain_a[li, g]),
                                             jnp.ones((d,), F32)]) for g in range(n_grp)]).reshape(1, cols_a)
    flag = jnp.tile(jnp.concatenate([jnp.ones((2 * d,), F32), jnp.zeros((d,), F32)]), n_grp).reshape(1, cols_a)
    qkv_s = _proj(xs, norm_mix_a[li], w_in, gain, flag, batch=1, dil=1, tn=d, col0=0, ncols=cols_a)
    qkv_s = qkv_s.reshape(n_dec, cols_a)
    no_sink = jnp.zeros((N_HEADS,), F32)
    o_p, l_p = [], []
    for g, (window, dil) in enumerate(A_GROUPS):
        qkv_g = _proj(xp, norm_mix_a[li], w_in, gain, flag, batch=batch, dil=dil, tn=d, col0=3 * g, ncols=3 * d)
        o, l = _band_attention(qkv_g, no_sink, q_blk=0, k_blk=1, v_blk=2, kv_heads=N_HEADS, has_sink=False)
        o_p.append(o)
        l_p.append(l)
        outs.append(_kept_rows(qkv_g, min(window, s_len), d, d, N_HEADS)[None])
        kv_s = qkv_s[:, (3 * g + 1) * d:(3 * g + 3) * d]
        outs.append(kv_s.reshape(1, n_dec, 1, 2, N_HEADS, HEAD_DIM).astype(F32))
    w_out = w_out_a[li].astype(BF16)
    xp = _merge_out(o_p, l_p, w_out, xp)
    xs = _out_proj(_step_attention_a(qkv_s.astype(F32), caches_a, li), w_out, xs)
    wgu, wd = w_gu_dense[li].astype(BF16), w_down_dense[li].astype(BF16)
    xp = _ffn_dense(xp, norm_ffn_dense[li], wgu, wd)
    xs = _ffn_dense(xs, norm_ffn_dense[li], wgu, wd)

    w_in = w_in_b[li].astype(BF16)
    cols_b = w_in.shape[1]
    kvw = HKV_B * HEAD_DIM
    gain = jnp.concatenate([_tile_heads(q_gain_b[li]) * q_scale, jnp.tile(k_gain_b[li].astype(F32), HKV_B),
                            jnp.ones((kvw,), F32)]).reshape(1, cols_b)
    flag = jnp.concatenate([jnp.ones((d + kvw,), F32), jnp.zeros((kvw,), F32)]).reshape(1, cols_b)
    qkv_p = _proj(xp, norm_mix_b[li], w_in, gain, flag, batch=batch, dil=1, tn=cols_b, col0=0, ncols=cols_b)
    qkv_s = _proj(xs, norm_mix_b[li], w_in, gain, flag, batch=1, dil=1, tn=cols_b, col0=0, ncols=cols_b)
    qkv_s = qkv_s.reshape(n_dec, cols_b).astype(F32)
    sink = sink_b[li].astype(F32)
    o, _ = _band_attention(qkv_p, sink, q_blk=0, k_blk=d // kvw, v_blk=d // kvw + 1, kv_heads=HKV_B, has_sink=True)
    outs.append(_kept_rows(qkv_p, min(N_STEPS, s_len), d, kvw, HKV_B)[None])
    outs.append(qkv_s[:, d:].reshape(1, n_dec, 1, 2, HKV_B, HEAD_DIM))
    w_out = w_out_b[li].astype(BF16)
    xp = _out_proj(o.reshape(batch * s_len, d), w_out, xp)
    xs = _out_proj(_step_attention_b(qkv_s, cache_b, sink, li), w_out, xs)
    wgu, wd = w_gu_moe[li].astype(BF16), w_down_moe[li].astype(BF16)
    xp = _moe(xp, norm_ffn_moe[li], w_router[li], wgu, wd, 512)
    xs = _moe(xs, norm_ffn_moe[li], w_router[li], wgu, wd, 128)

    return (xp.reshape(batch, s_len, d), xs.reshape(n_dec, 1, d), *outs)
```

```python
import functools

import numpy as np
import jax
import jax.numpy as jnp
from jax import lax
from jax.experimental import pallas as pl
from jax.experimental.pallas import tpu as pltpu

F32 = jnp.float32
BF16 = jnp.bfloat16

D_MODEL = 1024
HEAD_DIM = 64
N_HEADS = 16
HKV_B = 2
A_GROUPS = ((128, 1), (512, 4), (2048, 16))
N_STEPS = 128
BLK = 128
N_EXPERTS = 8
RMS_EPS = 1e-6
MASKED = -1e30
LANES = 128
SUBLANES = 8
MXU_DIM = 256
VMEM_LIMIT = 56 * 1024 * 1024
D_TILES = D_MODEL // LANES

SLOPES = tuple(float(2.0 ** (-8.0 * (h + 1) / N_HEADS)) for h in range(N_HEADS))


def _cparams(sem):
    return pltpu.CompilerParams(dimension_semantics=sem, vmem_limit_bytes=VMEM_LIMIT)


def _rms(x, g):
    ms = jnp.mean(x * x, axis=-1, keepdims=True)
    return x * lax.rsqrt(ms + RMS_EPS) * g


def _proj_kernel(x_ref, g_ref, w_ref, gain_ref, flag_ref, seg_ref, o_ref, h_scr, *perm_scr, dil):
    @pl.when(pl.program_id(1) == 0)
    def _():
        h = _rms(x_ref[...], g_ref[...])
        if dil == 1:
            h_scr[...] = h.astype(BF16)
        else:
            scr = perm_scr[0]
            n = h.shape[0] // dil
            for c in range(h.shape[1] // LANES):
                cols = slice(c * LANES, (c + 1) * LANES)
                scr[c] = h[:, cols]
                for r in range(dil):
                    h_scr[r * n:(r + 1) * n, cols] = scr[c, pl.ds(r, n, stride=dil), :].astype(BF16)

    acc = jnp.dot(h_scr[...], w_ref[...], preferred_element_type=F32)
    sq = (acc * acc).astype(BF16)
    tm, tn = acc.shape
    ssq = jnp.concatenate(
        [jnp.dot(sq[:, c * MXU_DIM:(c + 1) * MXU_DIM], seg_ref[...], preferred_element_type=F32)
         for c in range(tn // MXU_DIM)], axis=1)
    nrm = acc * lax.rsqrt(ssq * (1.0 / HEAD_DIM) + RMS_EPS) * gain_ref[...]
    res = jnp.where(flag_ref[...] > 0.0, nrm, acc)
    for r in range(dil):
        o_ref[0, r] = res[r * (tm // dil):(r + 1) * (tm // dil)].astype(o_ref.dtype)


def _proj(x, g, w, gain, flag, *, batch, dil, tn, col0, ncols):
    m, d = x.shape
    s_len = m // batch
    tm = min(s_len, 1024)
    tpb = s_len // tm
    seg = jnp.asarray(np.kron(np.eye(MXU_DIM // HEAD_DIM), np.ones((HEAD_DIM, HEAD_DIM))), BF16)
    scratch = [pltpu.VMEM((tm, d), BF16)] + ([pltpu.VMEM((d // LANES, tm, LANES), F32)] if dil > 1 else [])
    return pl.pallas_call(
        functools.partial(_proj_kernel, dil=dil),
        grid=(m // tm, ncols // tn),
        in_specs=[
            pl.BlockSpec((tm, d), lambda i, j: (i, 0)),
            pl.BlockSpec((1, d), lambda i, j: (0, 0)),
            pl.BlockSpec((d, tn), lambda i, j: (0, col0 + j)),
            pl.BlockSpec((1, tn), lambda i, j: (0, col0 + j)),
            pl.BlockSpec((1, tn), lambda i, j: (0, col0 + j)),
            pl.BlockSpec((MXU_DIM, MXU_DIM), lambda i, j: (0, 0)),
        ],
        out_specs=pl.BlockSpec((1, dil, tm // dil, tn), lambda i, j: (i // tpb, 0, i % tpb, j)),
        out_shape=jax.ShapeDtypeStruct((batch, dil, s_len // dil, ncols), BF16),
        scratch_shapes=scratch,
        compiler_params=_cparams(("parallel", "arbitrary")),
        name="proj_qknorm_d%d" % dil,
    )(x, g.reshape(1, d), w, gain, flag, seg)


def _band_kernel(sink_ref, q_ref, kp_ref, kc_ref, vp_ref, vc_ref, o_ref, lse_ref, k_all, v_all, *, dil, qb,
                 kv_heads, has_sink):
    c = pl.program_id(2)
    k_all[0:BLK] = kp_ref[0, 0]
    k_all[BLK:] = kc_ref[0, 0]
    v_all[0:BLK] = vp_ref[0, 0]
    v_all[BLK:] = vc_ref[0, 0]

    qi = lax.broadcasted_iota(jnp.int32, (BLK, 2 * BLK), 0)
    sj = lax.broadcasted_iota(jnp.int32, (BLK, 2 * BLK), 1)
    dist = qi - sj + BLK
    base = jnp.where((dist >= 0) & (dist <= N_STEPS), -(dist.astype(F32) * float(dil)), MASKED)
    base_first = jnp.where(sj >= BLK, base, MASKED)
    low_k = lax.broadcasted_iota(jnp.int32, (2 * BLK, LANES), 1) < HEAD_DIM
    lane_q = lax.broadcasted_iota(jnp.int32, (BLK, LANES), 1)
    low_q = lane_q < HEAD_DIM
    zeros_k = jnp.zeros((2 * BLK, LANES), BF16)
    ones_a = jnp.where(low_k, 1.0, 0.0).astype(BF16)
    ones_b = jnp.where(low_k, 0.0, 1.0).astype(BF16)
    nt = (((1,), (1,)), ((), ()))

    for i in range(qb):
        rows = slice(i * BLK, (i + 1) * BLK)
        win = slice(i * BLK, (i + 2) * BLK)
        base_i = jnp.where(c == 0, base_first, base) if i == 0 else base
        lse_acc = jnp.zeros((BLK, LANES), F32)
        if kv_heads != N_HEADS:
            kw, vw = k_all[win, :], v_all[win, :]
            zero_half = jnp.zeros((2 * BLK, HEAD_DIM), BF16)
            shared = {}
            for kv in range(kv_heads):
                k_kv = kw[:, kv * HEAD_DIM:(kv + 1) * HEAD_DIM]
                v_kv = vw[:, kv * HEAD_DIM:(kv + 1) * HEAD_DIM]
                shared[kv] = tuple((jnp.concatenate([t, zero_half], axis=1), jnp.concatenate([zero_half, t], axis=1))
                                   for t in (k_kv, v_kv))
        for pair in range(N_HEADS // 2):
            pc = slice(pair * LANES, (pair + 1) * LANES)
            ha, hb = 2 * pair, 2 * pair + 1
            qp = q_ref[0, 0, rows, pc]
            if kv_heads == N_HEADS:
                kw, vw = k_all[win, pc], v_all[win, pc]
                k_a, k_b = jnp.where(low_k, kw, zeros_k), jnp.where(low_k, zeros_k, kw)
                v_a, v_b = jnp.where(low_k, vw, zeros_k), jnp.where(low_k, zeros_k, vw)
            else:
                (k_a, k_b), (v_a, v_b) = shared[ha // (N_HEADS // kv_heads)]
            s_a = lax.dot_general(qp, k_a, nt, preferred_element_type=F32) + SLOPES[ha] * base_i
            s_b = lax.dot_general(qp, k_b, nt, preferred_element_type=F32) + SLOPES[hb] * base_i
            m_a = jnp.max(s_a, axis=-1, keepdims=True)
            m_b = jnp.max(s_b, axis=-1, keepdims=True)
            if has_sink:
                m_a = jnp.maximum(m_a, sink_ref[ha])
                m_b = jnp.maximum(m_b, sink_ref[hb])
            p_a = jnp.exp(s_a - m_a).astype(BF16)
            p_b = jnp.exp(s_b - m_b).astype(BF16)
            res = (jnp.dot(p_a, jnp.concatenate([v_a, ones_a], axis=1), preferred_element_type=F32)
                   + jnp.dot(p_b, jnp.concatenate([v_b, ones_b], axis=1), preferred_element_type=F32))
            den = res[:, LANES:]
            m_pair = jnp.where(low_q, m_a, m_b)
            if has_sink:
                den = den + jnp.where(low_q, jnp.exp(sink_ref[ha] - m_a), jnp.exp(sink_ref[hb] - m_b))
            o_ref[0, 0, rows, pc] = (res[:, :LANES] * (1.0 / den)).astype(o_ref.dtype)
            lse_acc = jnp.where((lane_q & (HEAD_DIM - 1)) == pair, m_pair + jnp.log(den), lse_acc)
        lse_ref[0, 0, rows, :] = lse_acc


def _band_attention(qkv, sink, *, q_blk, k_blk, v_blk, kv_heads, has_sink):
    batch, dil, ls, _ = qkv.shape
    qb = min(4, ls // BLK)
    rows = BLK * qb
    kvw = kv_heads * HEAD_DIM
    kern = functools.partial(_band_kernel, dil=dil, qb=qb, kv_heads=kv_heads, has_sink=has_sink)
    prev = lambda b, r, c: jnp.maximum(c * qb - 1, 0)
    return pl.pallas_call(
        kern,
        grid=(batch, dil, ls // rows),
        in_specs=[
            pl.BlockSpec(memory_space=pltpu.SMEM),
            pl.BlockSpec((1, 1, rows, D_MODEL), lambda b, r, c: (b, r, c, q_blk)),
            pl.BlockSpec((1, 1, BLK, kvw), lambda b, r, c: (b, r, prev(b, r, c), k_blk)),
            pl.BlockSpec((1, 1, rows, kvw), lambda b, r, c: (b, r, c, k_blk)),
            pl.BlockSpec((1, 1, BLK, kvw), lambda b, r, c: (b, r, prev(b, r, c), v_blk)),
            pl.BlockSpec((1, 1, rows, kvw), lambda b, r, c: (b, r, c, v_blk)),
        ],
        out_specs=[pl.BlockSpec((1, 1, rows, D_MODEL), lambda b, r, c: (b, r, c, 0)),
                   pl.BlockSpec((1, 1, rows, LANES), lambda b, r, c: (b, r, c, 0))],
        out_shape=[jax.ShapeDtypeStruct((batch, dil, ls, D_MODEL), BF16),
                   jax.ShapeDtypeStruct((batch, dil, ls, LANES), F32)],
        scratch_shapes=[pltpu.VMEM((BLK + rows, kvw), BF16), pltpu.VMEM((BLK + rows, kvw), BF16)],
        compiler_params=_cparams(("parallel", "parallel", "arbitrary")),
        name="band_attn_d%d" % dil,
    )(sink, qkv, qkv, qkv, qkv, qkv)


def _lse_lane(h):
    return h // 2 + HEAD_DIM * (h % 2)


def _head_expand():
    e = np.zeros((LANES, D_MODEL), np.float32)
    for h in range(N_HEADS):
        e[_lse_lane(h), h * HEAD_DIM:(h + 1) * HEAD_DIM] = 1.0
    return jnp.asarray(e, BF16)


def _split_dot(a, b):
    hi = a.astype(BF16)
    lo = (a - hi.astype(F32)).astype(BF16)
    return jnp.dot(hi, b, preferred_element_type=F32) + jnp.dot(lo, b, preferred_element_type=F32)


def _merge_out_kernel(*refs, dils):
    o_refs, l_refs = refs[0:3], refs[3:6]
    e_ref, w_ref, x_ref, out_ref = refs[6:10]
    scr = refs[10:]
    tm = x_ref.shape[0]
    outs, lses = [], []
    k = 0
    for o_ref, l_ref, dil in zip(o_refs, l_refs, dils):
        if dil == 1:
            outs.append(o_ref[0, 0].astype(F32))
            lses.append(l_ref[0, 0])
        else:
            so, sl = scr[k], scr[k + 1]
            k += 2
            for r in range(dil):
                rows = pl.ds(r, tm // dil, stride=dil)
                sl[rows, :] = l_ref[0, r]
                for c in range(D_TILES):
                    so[c, rows, :] = o_ref[0, r, :, c * LANES:(c + 1) * LANES].astype(F32)
            outs.append(jnp.concatenate([so[c] for c in range(D_TILES)], axis=1))
            lses.append(sl[...])
    top = jnp.maximum(jnp.maximum(lses[0], lses[1]), lses[2])
    es = [jnp.exp(l - top) for l in lses]
    inv = 1.0 / (es[0] + es[1] + es[2])
    mix = sum(_split_dot(e * inv, e_ref[...]) * o for e, o in zip(es, outs))
    out_ref[...] = x_ref[...] + jnp.dot(mix.astype(BF16), w_ref[...], preferred_element_type=F32)


def _merge_out(outs, lses, w, x):
    m, d = x.shape
    batch = outs[0].shape[0]
    s_len = m // batch
    tm = min(s_len, 256)
    tpb = s_len // tm
    dils = tuple(o.shape[1] for o in outs)
    grp = lambda width: [pl.BlockSpec((1, dil, tm // dil, width), lambda i: (i // tpb, 0, i % tpb, 0))
                         for dil in dils]
    scratch = [s for dil in dils if dil > 1
               for s in (pltpu.VMEM((D_TILES, tm, LANES), F32), pltpu.VMEM((tm, LANES), F32))]
    return pl.pallas_call(
        functools.partial(_merge_out_kernel, dils=dils),
        grid=(m // tm,),
        in_specs=grp(d) + grp(LANES) + [pl.BlockSpec((LANES, d), lambda i: (0, 0)),
                                         pl.BlockSpec((d, d), lambda i: (0, 0)),
                                         pl.BlockSpec((tm, d), lambda i: (i, 0))],
        out_specs=pl.BlockSpec((tm, d), lambda i: (i, 0)),
        out_shape=jax.ShapeDtypeStruct((m, d), F32),
        scratch_shapes=scratch,
        compiler_params=_cparams(("parallel",)),
        name="merge_outproj",
    )(*outs, *lses, _head_expand(), w, x)


def _out_kernel(o_ref, w_ref, x_ref, out_ref):
    out_ref[...] = x_ref[...] + jnp.dot(o_ref[...].astype(BF16), w_ref[...], preferred_element_type=F32)


def _out_proj(o, w, x):
    m, d = x.shape
    tm = min(m, 1024)
    return pl.pallas_call(
        _out_kernel,
        grid=(m // tm,),
        in_specs=[pl.BlockSpec((tm, d), lambda i: (i, 0)), pl.BlockSpec((d, d), lambda i: (0, 0)),
                  pl.BlockSpec((tm, d), lambda i: (i, 0))],
        out_specs=pl.BlockSpec((tm, d), lambda i: (i, 0)),
        out_shape=jax.ShapeDtypeStruct((m, d), F32),
        compiler_params=_cparams(("parallel",)),
        name="outproj",
    )(o, w, x)


def _silu(g):
    return g * (1.0 / (1.0 + jnp.exp(-g)))


def _ffn_kernel(x_ref, g_ref, wgu_ref, wd_ref, out_ref, *, d_ff, chunk):
    x = x_ref[...]
    h = _rms(x, g_ref[...]).astype(BF16)
    acc = jnp.zeros(x.shape, F32)
    for c in range(d_ff // chunk):
        gate = jnp.dot(h, wgu_ref[:, c * chunk:(c + 1) * chunk], preferred_element_type=F32)
        up = jnp.dot(h, wgu_ref[:, d_ff + c * chunk:d_ff + (c + 1) * chunk], preferred_element_type=F32)
        act = (_silu(gate) * up).astype(BF16)
        acc = acc + jnp.dot(act, wd_ref[c * chunk:(c + 1) * chunk, :], preferred_element_type=F32)
    out_ref[...] = x + acc


def _ffn_dense(x, g, wgu, wd):
    m, d = x.shape
    tm = min(m, 512)
    d_ff = wd.shape[0]
    kern = functools.partial(_ffn_kernel, d_ff=d_ff, chunk=d_ff // 2)
    return pl.pallas_call(
        kern,
        grid=(m // tm,),
        in_specs=[pl.BlockSpec((tm, d), lambda i: (i, 0)), pl.BlockSpec((1, d), lambda i: (0, 0)),
                  pl.BlockSpec((d, 2 * d_ff), lambda i: (0, 0), pipeline_mode=pl.Buffered(1)),
                  pl.BlockSpec((d_ff, d), lambda i: (0, 0), pipeline_mode=pl.Buffered(1))],
        out_specs=pl.BlockSpec((tm, d), lambda i: (i, 0)),
        out_shape=jax.ShapeDtypeStruct((m, d), F32),
        compiler_params=_cparams(("parallel",)),
        name="ffn_dense",
    )(x, g.reshape(1, d), wgu, wd)


def _router_kernel(x_ref, g_ref, wr_ref, idx_ref, gate_ref):
    h = _rms(x_ref[...], g_ref[...])
    logits = jnp.dot(h, wr_ref[...], preferred_element_type=F32, precision=lax.Precision.HIGHEST)
    lane = lax.broadcasted_iota(jnp.int32, logits.shape, 1)
    logits = jnp.where(lane < N_EXPERTS, logits, -jnp.inf)
    t1 = jnp.max(logits, axis=-1, keepdims=True)
    i1 = jnp.min(jnp.where(logits == t1, lane, LANES), axis=-1, keepdims=True)
    rest = jnp.where(lane == i1, -jnp.inf, logits)
    t2 = jnp.max(rest, axis=-1, keepdims=True)
    i2 = jnp.min(jnp.where(rest == t2, lane, LANES), axis=-1, keepdims=True)
    e2 = jnp.exp(t2 - t1)
    g1 = 1.0 / (1.0 + e2)
    g2 = e2 / (1.0 + e2)
    idx_ref[...] = jnp.where(lane == 0, i1, jnp.where(lane == 1, i2, 0))
    gate_ref[...] = jnp.where(lane == 0, g1, jnp.where(lane == 1, g2, 0.0))


def _router(x, g, w_router):
    m, d = x.shape
    tm = min(m, 1024)
    wr = jnp.zeros((d, LANES), F32).at[:, :N_EXPERTS].set(w_router)
    return pl.pallas_call(
        _router_kernel,
        grid=(m // tm,),
        in_specs=[pl.BlockSpec((tm, d), lambda i: (i, 0)), pl.BlockSpec((1, d), lambda i: (0, 0)),
                  pl.BlockSpec((d, LANES), lambda i: (0, 0))],
        out_specs=[pl.BlockSpec((tm, LANES), lambda i: (i, 0))] * 2,
        out_shape=[jax.ShapeDtypeStruct((m, LANES), jnp.int32), jax.ShapeDtypeStruct((m, LANES), F32)],
        compiler_params=_cparams(("parallel",)),
        name="router_top2",
    )(x, g.reshape(1, d), wr)


ROW_UNROLL = 4


def _row_copy(src, dst, sem, src_row, dst_row):
    return pltpu.make_async_copy(src.at[pl.ds(src_row, 1)], dst.at[pl.ds(dst_row, 1)], sem)


ZERO_ROWS = 128


def _dispatch_kernel(pad_ref, pos_ref, x_ref, xs_hbm, zeros, sem, *, rows):
    @pl.when(pl.program_id(0) == 0)
    def _():
        zeros[...] = jnp.zeros(zeros.shape, zeros.dtype)

        def pad_start(r, carry):
            _row_copy(zeros, xs_hbm, sem, 0, r).start()
            return carry

        def pad_wait(r, carry):
            _row_copy(zeros, xs_hbm, sem, 0, r).wait()
            return carry

        for e in range(N_EXPERTS):
            lax.fori_loop(pad_ref[e], pad_ref[N_EXPERTS + e], pad_start, 0)
            lax.fori_loop(pad_ref[e], pad_ref[N_EXPERTS + e], pad_wait, 0)

        def tail_copy(b):
            return pltpu.make_async_copy(zeros, xs_hbm.at[pl.ds(b * ZERO_ROWS, ZERO_ROWS)], sem)

        def tail_start(b, carry):
            tail_copy(b).start()
            return carry

        def tail_wait(b, carry):
            tail_copy(b).wait()
            return carry

        first, last = pad_ref[2 * N_EXPERTS - 1] // ZERO_ROWS, xs_hbm.shape[0] // ZERO_ROWS
        lax.fori_loop(first, last, tail_start, 0)
        lax.fori_loop(first, last, tail_wait, 0)

    def start(t, carry):
        for u in range(ROW_UNROLL):
            r = t * ROW_UNROLL + u
            _row_copy(x_ref, xs_hbm, sem, r, pos_ref[0, 0, 2 * r]).start(priority=0)
            _row_copy(x_ref, xs_hbm, sem, r, pos_ref[0, 0, 2 * r + 1]).start(priority=1)
        return carry

    def wait(t, carry):
        for u in range(2 * ROW_UNROLL):
            _row_copy(x_ref, xs_hbm, sem, 0, 0).wait()
        return carry

    lax.fori_loop(0, rows // ROW_UNROLL, start, 0)
    lax.fori_loop(0, rows // ROW_UNROLL, wait, 0)


def _dispatch(x, pos, pad, n_rows, rows):
    m, d = x.shape
    assert n_rows % ZERO_ROWS == 0
    return pl.pallas_call(
        functools.partial(_dispatch_kernel, rows=rows),
        grid_spec=pltpu.PrefetchScalarGridSpec(
            num_scalar_prefetch=1,
            grid=(m // rows,),
            in_specs=[pl.BlockSpec((1, 1, 2 * rows), lambda i, pad_ref: (i, 0, 0), memory_space=pltpu.SMEM),
                      pl.BlockSpec((rows, d), lambda i, pad_ref: (i, 0))],
            out_specs=pl.BlockSpec(memory_space=pl.ANY),
            scratch_shapes=[pltpu.VMEM((ZERO_ROWS, d), F32), pltpu.SemaphoreType.DMA(())],
        ),
        out_shape=jax.ShapeDtypeStruct((n_rows, d), x.dtype),
        compiler_params=_cparams(("arbitrary",)),
        name="moe_dispatch",
    )(pad, pos.reshape(m // rows, 1, 2 * rows), x)


def _expert_kernel(te_ref, nv_ref, xs_ref, g_ref, wgu_ref, wd_ref, o_ref, *, d_ff):
    @pl.when(pl.program_id(0) < nv_ref[0])
    def _():
        h = _rms(xs_ref[...], g_ref[...]).astype(BF16)
        gate = jnp.dot(h, wgu_ref[0, :, :d_ff], preferred_element_type=F32)
        up = jnp.dot(h, wgu_ref[0, :, d_ff:], preferred_element_type=F32)
        act = (_silu(gate) * up).astype(BF16)
        o_ref[...] = jnp.dot(act, wd_ref[0], preferred_element_type=F32)

    @pl.when(pl.program_id(0) >= nv_ref[0])
    def _():
        o_ref[...] = jnp.zeros(o_ref.shape, o_ref.dtype)


def _experts(xs, g, tile_expert, n_valid, wgu, wd, tm):
    n, d = xs.shape
    d_ff = wd.shape[1]
    row_tile = lambda i, te, nv: (jnp.minimum(i, nv[0] - 1), 0)
    return pl.pallas_call(
        functools.partial(_expert_kernel, d_ff=d_ff),
        grid_spec=pltpu.PrefetchScalarGridSpec(
            num_scalar_prefetch=2,
            grid=(n // tm,),
            in_specs=[
                pl.BlockSpec((tm, d), row_tile),
                pl.BlockSpec((1, d), lambda i, te, nv: (0, 0)),
                pl.BlockSpec((1, d, 2 * d_ff), lambda i, te, nv: (te[i], 0, 0)),
                pl.BlockSpec((1, d_ff, d), lambda i, te, nv: (te[i], 0, 0)),
            ],
            out_specs=pl.BlockSpec((tm, d), lambda i, te, nv: (i, 0)),
        ),
        out_shape=jax.ShapeDtypeStruct((n, d), F32),
        compiler_params=_cparams(("arbitrary",)),
        name="moe_experts",
    )(tile_expert, n_valid, xs, g.reshape(1, d), wgu, wd)


def _combine_kernel(pos_ref, x_ref, gate_ref, y_hbm, o_ref, buf0, buf1, sem, *, rows):
    def start(t, carry):
        for u in range(ROW_UNROLL):
            r = t * ROW_UNROLL + u
            _row_copy(y_hbm, buf0, sem.at[0], pos_ref[0, 0, 2 * r], r).start(priority=0)
            _row_copy(y_hbm, buf1, sem.at[1], pos_ref[0, 0, 2 * r + 1], r).start(priority=1)
        return carry

    def wait(t, carry):
        for u in range(ROW_UNROLL):
            r = t * ROW_UNROLL + u
            _row_copy(y_hbm, buf0, sem.at[0], 0, r).wait()
            _row_copy(y_hbm, buf1, sem.at[1], 0, r).wait()
        return carry

    lax.fori_loop(0, rows // ROW_UNROLL, start, 0)
    lax.fori_loop(0, rows // ROW_UNROLL, wait, 0)
    o_ref[...] = x_ref[...] + (gate_ref[:, 0:1] * buf0[...] + gate_ref[:, 1:2] * buf1[...])


def _combine(x, ys, pos, gate, rows):
    m, d = x.shape
    return pl.pallas_call(
        functools.partial(_combine_kernel, rows=rows),
        grid=(m // rows,),
        in_specs=[pl.BlockSpec((1, 1, 2 * rows), lambda i: (i, 0, 0), memory_space=pltpu.SMEM),
                  pl.BlockSpec((rows, d), lambda i: (i, 0)), pl.BlockSpec((rows, LANES), lambda i: (i, 0)),
                  pl.BlockSpec(memory_space=pl.ANY)],
        out_specs=pl.BlockSpec((rows, d), lambda i: (i, 0)),
        scratch_shapes=[pltpu.VMEM((rows, d), F32), pltpu.VMEM((rows, d), F32), pltpu.SemaphoreType.DMA((2,))],
        out_shape=jax.ShapeDtypeStruct((m, d), F32),
        compiler_params=_cparams(("arbitrary",)),
        name="moe_combine",
    )(pos.reshape(m // rows, 1, 2 * rows), x, gate, ys)


def _moe(x, g, w_router, wgu, wd, tm):
    m = x.shape[0]
    idx, gate = _router(x, g, w_router)
    e_flat = idx[:, :2].reshape(-1)
    onehot = (e_flat[:, None] == jnp.arange(N_EXPERTS)[None, :]).astype(jnp.int32)
    counts = jnp.sum(onehot, axis=0)
    rank = jnp.sum((jnp.cumsum(onehot, axis=0) - onehot) * onehot, axis=1)
    padded = ((counts + tm - 1) // tm) * tm
    ends = jnp.cumsum(padded)
    starts = ends - padded
    pos = (jnp.sum(onehot * starts[None, :], axis=1) + rank).astype(jnp.int32)
    pad = jnp.concatenate([starts + counts, ends]).astype(jnp.int32)
    n_tiles = -(-(2 * m + N_EXPERTS * (tm - 1)) // tm)
    tile_start = jnp.arange(n_tiles, dtype=jnp.int32) * tm
    tile_expert = jnp.minimum(jnp.sum((tile_start[:, None] >= ends[None, :]).astype(jnp.int32), axis=1),
                              N_EXPERTS - 1).astype(jnp.int32)
    n_valid = (ends[-1] // tm).astype(jnp.int32).reshape(1)

    rows = min(m, 256)
    xs = _dispatch(x, pos, pad, n_tiles * tm, rows)
    ys = _experts(xs, g, tile_expert, n_valid, wgu, wd, tm)
    return _combine(x, ys, pos, gate, rows)


STEP_HB = 8


def _positions_last(cache):
    return cache.transpose(0, 1, 3, 4, 5, 2)


def _step_softmax(s, s_new, slope, sink, dil):
    length = s.shape[1]
    pos = lax.broadcasted_iota(jnp.int32, (1, length), 1)
    s = s - slope * (length - pos).astype(F32)
    if dil > 1:
        s = jnp.where((pos & (dil - 1)) == 0, s, MASKED)
    m = jnp.maximum(jnp.max(s, axis=1, keepdims=True), s_new)
    if sink is not None:
        m = jnp.maximum(m, sink)
    p = jnp.exp(s - m)
    p_new = jnp.exp(s_new - m)
    den = jnp.sum(p, axis=1, keepdims=True) + p_new
    if sink is not None:
        den = den + jnp.exp(sink - m)
    return p, p_new, 1.0 / den, m + jnp.log(den)


def _step_a_kernel(q_ref, c0_ref, c1_ref, c2_ref, slope_ref, o_ref):
    heads = range(STEP_HB)
    slope = slope_ref[...]
    outs, lses = [], []
    for g, (c_ref, (_, dil)) in enumerate(zip((c0_ref, c1_ref, c2_ref), A_GROUPS)):
        q = [q_ref[3 * g, :, i:i + 1] for i in heads]
        s = jnp.concatenate([jnp.sum(c_ref[0, i] * q[i], axis=0, keepdims=True) for i in heads], axis=0)
        s_new = jnp.concatenate([jnp.sum(q_ref[3 * g + 1, :, i:i + 1] * q[i], axis=0, keepdims=True)
                                 for i in heads], axis=0)
        p, p_new, inv, lse = _step_softmax(s, s_new, slope, None, dil)
        outs.append([(jnp.sum(c_ref[1, i] * p[i:i + 1, :], axis=1, keepdims=True)
                      + p_new[i:i + 1, :] * q_ref[3 * g + 2, :, i:i + 1]) * inv[i:i + 1, :] for i in heads])
        lses.append(lse)
    top = jnp.maximum(jnp.maximum(lses[0], lses[1]), lses[2])
    es = [jnp.exp(l - top) for l in lses]
    inv = 1.0 / (es[0] + es[1] + es[2])
    ws = [e * inv for e in es]
    o_ref[...] = jnp.concatenate(
        [sum(ws[g][i:i + 1, :] * outs[g][i] for g in range(len(A_GROUPS))) for i in heads], axis=1)


def _step_attention_a(qkv, caches, li):
    n = qkv.shape[0]
    hb = STEP_HB
    nhb = N_HEADS // hb
    n_seg = 3 * len(A_GROUPS)
    q_t = qkv.reshape(n, n_seg, nhb, hb, HEAD_DIM).transpose(0, 2, 1, 4, 3)
    slope = jnp.asarray(SLOPES, F32).reshape(nhb, hb, 1)
    cache_specs = [pl.BlockSpec((None, None, 2, hb, HEAD_DIM, c.shape[2]), lambda i, j: (li, i, 0, j, 0, 0))
                   for c in caches]
    out = pl.pallas_call(
        _step_a_kernel,
        grid=(n, nhb),
        in_specs=[pl.BlockSpec((None, None, n_seg, HEAD_DIM, hb), lambda i, j: (i, j, 0, 0, 0))] + cache_specs
        + [pl.BlockSpec((None, hb, 1), lambda i, j: (j, 0, 0))],
        out_specs=pl.BlockSpec((None, None, HEAD_DIM, hb), lambda i, j: (i, j, 0, 0)),
        out_shape=jax.ShapeDtypeStruct((n, nhb, HEAD_DIM, hb), F32),
        compiler_params=_cparams(("parallel", "parallel")),
        name="step_attn_a",
    )(q_t, *[_positions_last(c) for c in caches], slope)
    return out.transpose(0, 1, 3, 2).reshape(n, D_MODEL)


STEP_TB = 8


def _step_b_kernel(q_ref, kn_ref, vn_ref, c_ref, slope_ref, sink_ref, o_ref):
    rep = N_HEADS // HKV_B
    hi = lax.Precision.HIGHEST
    for t in range(q_ref.shape[0]):
        for kv in range(HKV_B):
            hs = slice(kv * rep, (kv + 1) * rep)
            q = q_ref[t, hs, :]
            s = jnp.dot(q, c_ref[t, 0, kv], preferred_element_type=F32, precision=hi)
            s_new = jnp.sum(q * kn_ref[t, kv:kv + 1, :], axis=1, keepdims=True)
            p, p_new, inv, _ = _step_softmax(s, s_new, slope_ref[hs, :], sink_ref[hs, :], 1)
            pv = lax.dot_general(p, c_ref[t, 1, kv], (((1,), (1,)), ((), ())), preferred_element_type=F32,
                                 precision=hi)
            o_ref[t, hs, :] = (pv + p_new * vn_ref[t, kv:kv + 1, :]) * inv


def _step_attention_b(qkv, cache, sink, li):
    n = qkv.shape[0]
    kvw = HKV_B * HEAD_DIM
    tb = min(n, STEP_TB)
    col = lambda v: v.astype(F32).reshape(N_HEADS, 1)
    tok = lambda h: pl.BlockSpec((tb, h, HEAD_DIM), lambda i: (i, 0, 0))
    out = pl.pallas_call(
        _step_b_kernel,
        grid=(n // tb,),
        in_specs=[tok(N_HEADS), tok(HKV_B), tok(HKV_B),
                  pl.BlockSpec((None, tb, 2, HKV_B, HEAD_DIM, cache.shape[2]), lambda i: (li, i, 0, 0, 0, 0)),
                  pl.BlockSpec((N_HEADS, 1), lambda i: (0, 0)), pl.BlockSpec((N_HEADS, 1), lambda i: (0, 0))],
        out_specs=tok(N_HEADS),
        out_shape=jax.ShapeDtypeStruct((n, N_HEADS, HEAD_DIM), F32),
        compiler_params=_cparams(("parallel",)),
        name="step_attn_b",
    )(qkv[:, :D_MODEL].reshape(n, N_HEADS, HEAD_DIM), qkv[:, D_MODEL:D_MODEL + kvw].reshape(n, HKV_B, HEAD_DIM),
      qkv[:, D_MODEL + kvw:].reshape(n, HKV_B, HEAD_DIM), _positions_last(cache), col(jnp.asarray(SLOPES, F32)),
      col(sink))
    return out.reshape(n, D_MODEL)


def _tile_heads(v):
    return jnp.tile(v.astype(F32), N_HEADS)


def _kept_rows(qkv, keep, k_col, width, heads):
    batch, dil, ls, _ = qkv.shape
    kv = qkv[:, :, ls - keep // dil:, k_col:k_col + 2 * width]
    kv = kv.transpose(0, 2, 1, 3).reshape(batch, keep, 2, heads, HEAD_DIM)
    return kv.astype(F32)


def kernel(x_prompt, x_sample, cache_a_w128, cache_a_w512, cache_a_w2048, cache_b, norm_mix_a, w_in_a, q_gain_a, k_gain_a, w_out_a, norm_ffn_dense, w_gu_dense, w_down_dense, norm_mix_b, w_in_b, q_gain_b, k_gain_b, sink_b, w_out_b, norm_ffn_moe, w_router, w_gu_moe, w_down_moe):
    batch, s_len, d = x_prompt.shape
    n_dec = x_sample.shape[0]
    assert x_sample.shape[1] == 1 and d == D_MODEL
    caches_a = (cache_a_w128, cache_a_w512, cache_a_w2048)
    xp = x_prompt.reshape(batch * s_len, d)
    xs = x_sample.reshape(n_dec, d)
    q_scale = HEAD_DIM ** -0.5
    n_grp = len(A_GROUPS)
    outs = []

    li = 0
    w_in = w_in_a[li].astype(BF16)
    cols_a = w_in.shape[1]
    gain = jnp.concatenate([jnp.concatenate([_tile_heads(q_gain_a[li, g]) * q_scale, _tile_heads(k_gain_a[li, g]),
                                             jnp.ones((d,), F32)]) for g in range(n_grp)]).reshape(1, cols_a)
    flag = jnp.tile(jnp.concatenate([jnp.ones((2 * d,), F32), jnp.zeros((d,), F32)]), n_grp).reshape(1, cols_a)
    qkv_s = _proj(xs, norm_mix_a[li], w_in, gain, flag, batch=1, dil=1, tn=d, col0=0, ncols=cols_a)
    qkv_s = qkv_s.reshape(n_dec, cols_a)
    no_sink = jnp.zeros((N_HEADS,), F32)
    o_p, l_p = [], []
    for g, (window, dil) in enumerate(A_GROUPS):
        qkv_g = _proj(xp, norm_mix_a[li], w_in, gain, flag, batch=batch, dil=dil, tn=d, col0=3 * g, ncols=3 * d)
        o, l = _band_attention(qkv_g, no_sink, q_blk=0, k_blk=1, v_blk=2, kv_heads=N_HEADS, has_sink=False)
        o_p.append(o)
        l_p.append(l)
        outs.append(_kept_rows(qkv_g, min(window, s_len), d, d, N_HEADS)[None])
        kv_s = qkv_s[:, (3 * g + 1) * d:(3 * g + 3) * d]
        outs.append(kv_s.reshape(1, n_dec, 1, 2, N_HEADS, HEAD_DIM).astype(F32))
    w_out = w_out_a[li].astype(BF16)
    xp = _merge_out(o_p, l_p, w_out, xp)
    xs = _out_proj(_step_attention_a(qkv_s.astype(F32), caches_a, li), w_out, xs)
    wgu, wd = w_gu_dense[li].astype(BF16), w_down_dense[li].astype(BF16)
    xp = _ffn_dense(xp, norm_ffn_dense[li], wgu, wd)
    xs = _ffn_dense(xs, norm_ffn_dense[li], wgu, wd)

    w_in = w_in_b[li].astype(BF16)
    cols_b = w_in.shape[1]
    kvw = HKV_B * HEAD_DIM
    gain = jnp.concatenate([_tile_heads(q_gain_b[li]) * q_scale, jnp.tile(k_gain_b[li].astype(F32), HKV_B),
                            jnp.ones((kvw,), F32)]).reshape(1, cols_b)
    flag = jnp.concatenate([jnp.ones((d + kvw,), F32), jnp.zeros((kvw,), F32)]).reshape(1, cols_b)
    qkv_p = _proj(xp, norm_mix_b[li], w_in, gain, flag, batch=batch, dil=1, tn=cols_b, col0=0, ncols=cols_b)
    qkv_s = _proj(xs, norm_mix_b[li], w_in, gain, flag, batch=1, dil=1, tn=cols_b, col0=0, ncols=cols_b)
    qkv_s = qkv_s.reshape(n_dec, cols_b).astype(F32)
    sink = sink_b[li].astype(F32)
    o, _ = _band_attention(qkv_p, sink, q_blk=0, k_blk=d // kvw, v_blk=d // kvw + 1, kv_heads=HKV_B, has_sink=True)
    outs.append(_kept_rows(qkv_p, min(N_STEPS, s_len), d, kvw, HKV_B)[None])
    outs.append(qkv_s[:, d:].reshape(1, n_dec, 1, 2, HKV_B, HEAD_DIM))
    w_out = w_out_b[li].astype(BF16)
    xp = _out_proj(o.reshape(batch * s_len, d), w_out, xp)
    xs = _out_proj(_step_attention_b(qkv_s, cache_b, sink, li), w_out, xs)
    wgu, wd = w_gu_moe[li].astype(BF16), w_down_moe[li].astype(BF16)
    xp = _moe(xp, norm_ffn_moe[li], w_router[li], wgu, wd, 512)
    xs = _moe(xs, norm_ffn_moe[li], w_router[li], wgu, wd, 128)

    return (xp.reshape(batch, s_len, d), xs.reshape(n_dec, 1, d), *outs)
```

```python
import functools

import numpy as np
import jax
import jax.numpy as jnp
from jax import lax
from jax.experimental import pallas as pl
from jax.experimental.pallas import tpu as pltpu

F32 = jnp.float32
BF16 = jnp.bfloat16

D_MODEL = 1024
HEAD_DIM = 64
N_HEADS = 16
HKV_B = 2
A_GROUPS = ((128, 1), (512, 4), (2048, 16))
N_STEPS = 128
BLK = 128
N_EXPERTS = 8
RMS_EPS = 1e-6
MASKED = -1e30
LANES = 128
SUBLANES = 8
MXU_DIM = 256
VMEM_LIMIT = 56 * 1024 * 1024
D_TILES = D_MODEL // LANES

SLOPES = tuple(float(2.0 ** (-8.0 * (h + 1) / N_HEADS)) for h in range(N_HEADS))


def _cparams(sem):
    return pltpu.CompilerParams(dimension_semantics=sem, vmem_limit_bytes=VMEM_LIMIT)


def _rms(x, g):
    ms = jnp.mean(x * x, axis=-1, keepdims=True)
    return x * lax.rsqrt(ms + RMS_EPS) * g


def _proj_kernel(x_ref, g_ref, w_ref, gain_ref, flag_ref, seg_ref, o_ref, h_scr, *perm_scr, dil):
    @pl.when(pl.program_id(1) == 0)
    def _():
        h = _rms(x_ref[...], g_ref[...])
        if dil == 1:
            h_scr[...] = h.astype(BF16)
        else:
            scr = perm_scr[0]
            n = h.shape[0] // dil
            for c in range(h.shape[1] // LANES):
                cols = slice(c * LANES, (c + 1) * LANES)
                scr[c] = h[:, cols]
                for r in range(dil):
                    h_scr[r * n:(r + 1) * n, cols] = scr[c, pl.ds(r, n, stride=dil), :].astype(BF16)

    acc = jnp.dot(h_scr[...], w_ref[...], preferred_element_type=F32)
    sq = (acc * acc).astype(BF16)
    tm, tn = acc.shape
    ssq = jnp.concatenate(
        [jnp.dot(sq[:, c * MXU_DIM:(c + 1) * MXU_DIM], seg_ref[...], preferred_element_type=F32)
         for c in range(tn // MXU_DIM)], axis=1)
    nrm = acc * lax.rsqrt(ssq * (1.0 / HEAD_DIM) + RMS_EPS) * gain_ref[...]
    res = jnp.where(flag_ref[...] > 0.0, nrm, acc)
    for r in range(dil):
        o_ref[0, r] = res[r * (tm // dil):(r + 1) * (tm // dil)].astype(o_ref.dtype)


def _proj(x, g, w, gain, flag, *, batch, dil, tn, col0, ncols):
    m, d = x.shape
    s_len = m // batch
    tm = min(s_len, 1024)
    tpb = s_len // tm
    seg = jnp.asarray(np.kron(np.eye(MXU_DIM // HEAD_DIM), np.ones((HEAD_DIM, HEAD_DIM))), BF16)
    scratch = [pltpu.VMEM((tm, d), BF16)] + ([pltpu.VMEM((d // LANES, tm, LANES), F32)] if dil > 1 else [])
    return pl.pallas_call(
        functools.partial(_proj_kernel, dil=dil),
        grid=(m // tm, ncols // tn),
        in_specs=[
            pl.BlockSpec((tm, d), lambda i, j: (i, 0)),
            pl.BlockSpec((1, d), lambda i, j: (0, 0)),
            pl.BlockSpec((d, tn), lambda i, j: (0, col0 + j)),
            pl.BlockSpec((1, tn), lambda i, j: (0, col0 + j)),
            pl.BlockSpec((1, tn), lambda i, j: (0, col0 + j)),
            pl.BlockSpec((MXU_DIM, MXU_DIM), lambda i, j: (0, 0)),
        ],
        out_specs=pl.BlockSpec((1, dil, tm // dil, tn), lambda i, j: (i // tpb, 0, i % tpb, j)),
        out_shape=jax.ShapeDtypeStruct((batch, dil, s_len // dil, ncols), BF16),
        scratch_shapes=scratch,
        compiler_params=_cparams(("parallel", "arbitrary")),
        name="proj_qknorm_d%d" % dil,
    )(x, g.reshape(1, d), w, gain, flag, seg)


def _band_kernel(sink_ref, q_ref, kp_ref, kc_ref, vp_ref, vc_ref, o_ref, lse_ref, k_all, v_all, *, dil, qb,
                 kv_heads, has_sink):
    c = pl.program_id(2)
    k_all[0:BLK] = kp_ref[0, 0]
    k_all[BLK:] = kc_ref[0, 0]
    v_all[0:BLK] = vp_ref[0, 0]
    v_all[BLK:] = vc_ref[0, 0]

    qi = lax.broadcasted_iota(jnp.int32, (BLK, 2 * BLK), 0)
    sj = lax.broadcasted_iota(jnp.int32, (BLK, 2 * BLK), 1)
    dist = qi - sj + BLK
    base = jnp.where((dist >= 0) & (dist <= N_STEPS), -(dist.astype(F32) * float(dil)), MASKED)
    base_first = jnp.where(sj >= BLK, base, MASKED)
    low_k = lax.broadcasted_iota(jnp.int32, (2 * BLK, LANES), 1) < HEAD_DIM
    lane_q = lax.broadcasted_iota(jnp.int32, (BLK, LANES), 1)
    low_q = lane_q < HEAD_DIM
    zeros_k = jnp.zeros((2 * BLK, LANES), BF16)
    ones_a = jnp.where(low_k, 1.0, 0.0).astype(BF16)
    ones_b = jnp.where(low_k, 0.0, 1.0).astype(BF16)
    nt = (((1,), (1,)), ((), ()))

    for i in range(qb):
        rows = slice(i * BLK, (i + 1) * BLK)
        win = slice(i * BLK, (i + 2) * BLK)
        base_i = jnp.where(c == 0, base_first, base) if i == 0 else base
        lse_acc = jnp.zeros((BLK, LANES), F32)
        if kv_heads != N_HEADS:
            kw, vw = k_all[win, :], v_all[win, :]
            zero_half = jnp.zeros((2 * BLK, HEAD_DIM), BF16)
            shared = {}
            for kv in range(kv_heads):
                k_kv = kw[:, kv * HEAD_DIM:(kv + 1) * HEAD_DIM]
                v_kv = vw[:, kv * HEAD_DIM:(kv + 1) * HEAD_DIM]
                shared[kv] = tuple((jnp.concatenate([t, zero_half], axis=1), jnp.concatenate([zero_half, t], axis=1))
                                   for t in (k_kv, v_kv))
        for pair in range(N_HEADS // 2):
            pc = slice(pair * LANES, (pair + 1) * LANES)
            ha, hb = 2 * pair, 2 * pair + 1
            qp = q_ref[0, 0, rows, pc]
            if kv_heads == N_HEADS:
                kw, vw = k_all[win, pc], v_all[win, pc]
                k_a, k_b = jnp.where(low_k, kw, zeros_k), jnp.where(low_k, zeros_k, kw)
                v_a, v_b = jnp.where(low_k, vw, zeros_k), jnp.where(low_k, zeros_k, vw)
            else:
                (k_a, k_b), (v_a, v_b) = shared[ha // (N_HEADS // kv_heads)]
            s_a = lax.dot_general(qp, k_a, nt, preferred_element_type=F32) + SLOPES[ha] * base_i
            s_b = lax.dot_general(qp, k_b, nt, preferred_element_type=F32) + SLOPES[hb] * base_i
            m_a = jnp.max(s_a, axis=-1, keepdims=True)
            m_b = jnp.max(s_b, axis=-1, keepdims=True)
            if has_sink:
                m_a = jnp.maximum(m_a, sink_ref[ha])
                m_b = jnp.maximum(m_b, sink_ref[hb])
            p_a = jnp.exp(s_a - m_a).astype(BF16)
            p_b = jnp.exp(s_b - m_b).astype(BF16)
            res = (jnp.dot(p_a, jnp.concatenate([v_a, ones_a], axis=1), preferred_element_type=F32)
                   + jnp.dot(p_b, jnp.concatenate([v_b, ones_b], axis=1), preferred_element_type=F32))
            den = res[:, LANES:]
            m_pair = jnp.where(low_q, m_a, m_b)
            if has_sink:
                den = den + jnp.where(low_q, jnp.exp(sink_ref[ha] - m_a), jnp.exp(sink_ref[hb] - m_b))
            o_ref[0, 0, rows, pc] = (res[:, :LANES] * (1.0 / den)).astype(o_ref.dtype)
            lse_acc = jnp.where((lane_q & (HEAD_DIM - 1)) == pair, m_pair + jnp.log(den), lse_acc)
        lse_ref[0, 0, rows, :] = lse_acc


def _band_attention(qkv, sink, *, q_blk, k_blk, v_blk, kv_heads, has_sink):
    batch, dil, ls, _ = qkv.shape
    qb = min(4, ls // BLK)
    rows = BLK * qb
    kvw = kv_heads * HEAD_DIM
    kern = functools.partial(_band_kernel, dil=dil, qb=qb, kv_heads=kv_heads, has_sink=has_sink)
    prev = lambda b, r, c: jnp.maximum(c * qb - 1, 0)
    return pl.pallas_call(
        kern,
        grid=(batch, dil, ls // rows),
        in_specs=[
            pl.BlockSpec(memory_space=pltpu.SMEM),
            pl.BlockSpec((1, 1, rows, D_MODEL), lambda b, r, c: (b, r, c, q_blk)),
            pl.BlockSpec((1, 1, BLK, kvw), lambda b, r, c: (b, r, prev(b, r, c), k_blk)),
            pl.BlockSpec((1, 1, rows, kvw), lambda b, r, c: (b, r, c, k_blk)),
            pl.BlockSpec((1, 1, BLK, kvw), lambda b, r, c: (b, r, prev(b, r, c), v_blk)),
            pl.BlockSpec((1, 1, rows, kvw), lambda b, r, c: (b, r, c, v_blk)),
        ],
        out_specs=[pl.BlockSpec((1, 1, rows, D_MODEL), lambda b, r, c: (b, r, c, 0)),
                   pl.BlockSpec((1, 1, rows, LANES), lambda b, r, c: (b, r, c, 0))],
        out_shape=[jax.ShapeDtypeStruct((batch, dil, ls, D_MODEL), BF16),
                   jax.ShapeDtypeStruct((batch, dil, ls, LANES), F32)],
        scratch_shapes=[pltpu.VMEM((BLK + rows, kvw), BF16), pltpu.VMEM((BLK + rows, kvw), BF16)],
        compiler_params=_cparams(("parallel", "parallel", "arbitrary")),
        name="band_attn_d%d" % dil,
    )(sink, qkv, qkv, qkv, qkv, qkv)


def _lse_lane(h):
    return h // 2 + HEAD_DIM * (h % 2)


def _head_expand():
    e = np.zeros((LANES, D_MODEL), np.float32)
    for h in range(N_HEADS):
        e[_lse_lane(h), h * HEAD_DIM:(h + 1) * HEAD_DIM] = 1.0
    return jnp.asarray(e, BF16)


def _split_dot(a, b):
    hi = a.astype(BF16)
    lo = (a - hi.astype(F32)).astype(BF16)
    return jnp.dot(hi, b, preferred_element_type=F32) + jnp.dot(lo, b, preferred_element_type=F32)


def _merge_out_kernel(*refs, dils):
    o_refs, l_refs = refs[0:3], refs[3:6]
    e_ref, w_ref, x_ref, out_ref = refs[6:10]
    scr = refs[10:]
    tm = x_ref.shape[0]
    outs, lses = [], []
    k = 0
    for o_ref, l_ref, dil in zip(o_refs, l_refs, dils):
        if dil == 1:
            outs.append(o_ref[0, 0].astype(F32))
            lses.append(l_ref[0, 0])
        else:
            so, sl = scr[k], scr[k + 1]
            k += 2
            for r in range(dil):
                rows = pl.ds(r, tm // dil, stride=dil)
                sl[rows, :] = l_ref[0, r]
                for c in range(D_TILES):
                    so[c, rows, :] = o_ref[0, r, :, c * LANES:(c + 1) * LANES].astype(F32)
            outs.append(jnp.concatenate([so[c] for c in range(D_TILES)], axis=1))
            lses.append(sl[...])
    top = jnp.maximum(jnp.maximum(lses[0], lses[1]), lses[2])
    es = [jnp.exp(l - top) for l in lses]
    inv = 1.0 / (es[0] + es[1] + es[2])
    mix = sum(_split_dot(e * inv, e_ref[...]) * o for e, o in zip(es, outs))
    out_ref[...] = x_ref[...] + jnp.dot(mix.astype(BF16), w_ref[...], preferred_element_type=F32)


def _merge_out(outs, lses, w, x):
    m, d = x.shape
    batch = outs[0].shape[0]
    s_len = m // batch
    tm = min(s_len, 256)
    tpb = s_len // tm
    dils = tuple(o.shape[1] for o in outs)
    grp = lambda width: [pl.BlockSpec((1, dil, tm // dil, width), lambda i: (i // tpb, 0, i % tpb, 0))
                         for dil in dils]
    scratch = [s for dil in dils if dil > 1
               for s in (pltpu.VMEM((D_TILES, tm, LANES), F32), pltpu.VMEM((tm, LANES), F32))]
    return pl.pallas_call(
        functools.partial(_merge_out_kernel, dils=dils),
        grid=(m // tm,),
        in_specs=grp(d) + grp(LANES) + [pl.BlockSpec((LANES, d), lambda i: (0, 0)),
                                         pl.BlockSpec((d, d), lambda i: (0, 0)),
                                         pl.BlockSpec((tm, d), lambda i: (i, 0))],
        out_specs=pl.BlockSpec((tm, d), lambda i: (i, 0)),
        out_shape=jax.ShapeDtypeStruct((m, d), F32),
        scratch_shapes=scratch,
        compiler_params=_cparams(("parallel",)),
        name="merge_outproj",
    )(*outs, *lses, _head_expand(), w, x)


def _out_kernel(o_ref, w_ref, x_ref, out_ref):
    out_ref[...] = x_ref[...] + jnp.dot(o_ref[...].astype(BF16), w_ref[...], preferred_element_type=F32)


def _out_proj(o, w, x):
    m, d = x.shape
    tm = min(m, 1024)
    return pl.pallas_call(
        _out_kernel,
        grid=(m // tm,),
        in_specs=[pl.BlockSpec((tm, d), lambda i: (i, 0)), pl.BlockSpec((d, d), lambda i: (0, 0)),
                  pl.BlockSpec((tm, d), lambda i: (i, 0))],
        out_specs=pl.BlockSpec((tm, d), lambda i: (i, 0)),
        out_shape=jax.ShapeDtypeStruct((m, d), F32),
        compiler_params=_cparams(("parallel",)),
        name="outproj",
    )(o, w, x)


def _silu(g):
    return g * (1.0 / (1.0 + jnp.exp(-g)))


def _ffn_kernel(x_ref, g_ref, wgu_ref, wd_ref, out_ref, *, d_ff, chunk):
    x = x_ref[...]
    h = _rms(x, g_ref[...]).astype(BF16)
    acc = jnp.zeros(x.shape, F32)
    for c in range(d_ff // chunk):
        gate = jnp.dot(h, wgu_ref[:, c * chunk:(c + 1) * chunk], preferred_element_type=F32)
        up = jnp.dot(h, wgu_ref[:, d_ff + c * chunk:d_ff + (c + 1) * chunk], preferred_element_type=F32)
        act = (_silu(gate) * up).astype(BF16)
        acc = acc + jnp.dot(act, wd_ref[c * chunk:(c + 1) * chunk, :], preferred_element_type=F32)
    out_ref[...] = x + acc


def _ffn_dense(x, g, wgu, wd):
    m, d = x.shape
    tm = min(m, 512)
    d_ff = wd.shape[0]
    kern = functools.partial(_ffn_kernel, d_ff=d_ff, chunk=d_ff // 2)
    return pl.pallas_call(
        kern,
        grid=(m // tm,),
        in_specs=[pl.BlockSpec((tm, d), lambda i: (i, 0)), pl.BlockSpec((1, d), lambda i: (0, 0)),
                  pl.BlockSpec((d, 2 * d_ff), lambda i: (0, 0), pipeline_mode=pl.Buffered(1)),
                  pl.BlockSpec((d_ff, d), lambda i: (0, 0), pipeline_mode=pl.Buffered(1))],
        out_specs=pl.BlockSpec((tm, d), lambda i: (i, 0)),
        out_shape=jax.ShapeDtypeStruct((m, d), F32),
        compiler_params=_cparams(("parallel",)),
        name="ffn_dense",
    )(x, g.reshape(1, d), wgu, wd)


def _router_kernel(x_ref, g_ref, wr_ref, idx_ref, gate_ref):
    h = _rms(x_ref[...], g_ref[...])
    logits = jnp.dot(h, wr_ref[...], preferred_element_type=F32, precision=lax.Precision.HIGHEST)
    lane = lax.broadcasted_iota(jnp.int32, logits.shape, 1)
    logits = jnp.where(lane < N_EXPERTS, logits, -jnp.inf)
    t1 = jnp.max(logits, axis=-1, keepdims=True)
    i1 = jnp.min(jnp.where(logits == t1, lane, LANES), axis=-1, keepdims=True)
    rest = jnp.where(lane == i1, -jnp.inf, logits)
    t2 = jnp.max(rest, axis=-1, keepdims=True)
    i2 = jnp.min(jnp.where(rest == t2, lane, LANES), axis=-1, keepdims=True)
    e2 = jnp.exp(t2 - t1)
    g1 = 1.0 / (1.0 + e2)
    g2 = e2 / (1.0 + e2)
    idx_ref[...] = jnp.where(lane == 0, i1, jnp.where(lane == 1, i2, 0))
    gate_ref[...] = jnp.where(lane == 0, g1, jnp.where(lane == 1, g2, 0.0))


def _router(x, g, w_router):
    m, d = x.shape
    tm = min(m, 1024)
    wr = jnp.zeros((d, LANES), F32).at[:, :N_EXPERTS].set(w_router)
    return pl.pallas_call(
        _router_kernel,
        grid=(m // tm,),
        in_specs=[pl.BlockSpec((tm, d), lambda i: (i, 0)), pl.BlockSpec((1, d), lambda i: (0, 0)),
                  pl.BlockSpec((d, LANES), lambda i: (0, 0))],
        out_specs=[pl.BlockSpec((tm, LANES), lambda i: (i, 0))] * 2,
        out_shape=[jax.ShapeDtypeStruct((m, LANES), jnp.int32), jax.ShapeDtypeStruct((m, LANES), F32)],
        compiler_params=_cparams(("parallel",)),
        name="router_top2",
    )(x, g.reshape(1, d), wr)


ROW_UNROLL = 4


def _row_copy(src, dst, sem, src_row, dst_row):
    return pltpu.make_async_copy(src.at[pl.ds(src_row, 1)], dst.at[pl.ds(dst_row, 1)], sem)


ZERO_ROWS = 128


def _dispatch_kernel(pad_ref, pos_ref, x_ref, xs_hbm, zeros, sem, *, rows):
    @pl.when(pl.program_id(0) == 0)
    def _():
        zeros[...] = jnp.zeros(zeros.shape, zeros.dtype)

        def pad_start(r, carry):
            _row_copy(zeros, xs_hbm, sem, 0, r).start()
            return carry

        def pad_wait(r, carry):
            _row_copy(zeros, xs_hbm, sem, 0, r).wait()
            return carry

        for e in range(N_EXPERTS):
            lax.fori_loop(pad_ref[e], pad_ref[N_EXPERTS + e], pad_start, 0)
            lax.fori_loop(pad_ref[e], pad_ref[N_EXPERTS + e], pad_wait, 0)

        def tail_copy(b):
            return pltpu.make_async_copy(zeros, xs_hbm.at[pl.ds(b * ZERO_ROWS, ZERO_ROWS)], sem)

        def tail_start(b, carry):
            tail_copy(b).start()
            return carry

        def tail_wait(b, carry):
            tail_copy(b).wait()
            return carry

        first, last = pad_ref[2 * N_EXPERTS - 1] // ZERO_ROWS, xs_hbm.shape[0] // ZERO_ROWS
        lax.fori_loop(first, last, tail_start, 0)
        lax.fori_loop(first, last, tail_wait, 0)

    def start(t, carry):
        for u in range(ROW_UNROLL):
            r = t * ROW_UNROLL + u
            _row_copy(x_ref, xs_hbm, sem, r, pos_ref[0, 0, 2 * r]).start(priority=0)
            _row_copy(x_ref, xs_hbm, sem, r, pos_ref[0, 0, 2 * r + 1]).start(priority=1)
        return carry

    def wait(t, carry):
        for u in range(2 * ROW_UNROLL):
            _row_copy(x_ref, xs_hbm, sem, 0, 0).wait()
        return carry

    lax.fori_loop(0, rows // ROW_UNROLL, start, 0)
    lax.fori_loop(0, rows // ROW_UNROLL, wait, 0)


def _dispatch(x, pos, pad, n_rows, rows):
    m, d = x.shape
    assert n_rows % ZERO_ROWS == 0
    return pl.pallas_call(
        functools.partial(_dispatch_kernel, rows=rows),
        grid_spec=pltpu.PrefetchScalarGridSpec(
            num_scalar_prefetch=1,
            grid=(m // rows,),
            in_specs=[pl.BlockSpec((1, 1, 2 * rows), lambda i, pad_ref: (i, 0, 0), memory_space=pltpu.SMEM),
                      pl.BlockSpec((rows, d), lambda i, pad_ref: (i, 0))],
            out_specs=pl.BlockSpec(memory_space=pl.ANY),
            scratch_shapes=[pltpu.VMEM((ZERO_ROWS, d), F32), pltpu.SemaphoreType.DMA(())],
        ),
        out_shape=jax.ShapeDtypeStruct((n_rows, d), x.dtype),
        compiler_params=_cparams(("arbitrary",)),
        name="moe_dispatch",
    )(pad, pos.reshape(m // rows, 1, 2 * rows), x)


def _expert_kernel(te_ref, nv_ref, xs_ref, g_ref, wgu_ref, wd_ref, o_ref, *, d_ff):
    @pl.when(pl.program_id(0) < nv_ref[0])
    def _():
        h = _rms(xs_ref[...], g_ref[...]).astype(BF16)
        gate = jnp.dot(h, wgu_ref[0, :, :d_ff], preferred_element_type=F32)
        up = jnp.dot(h, wgu_ref[0, :, d_ff:], preferred_element_type=F32)
        act = (_silu(gate) * up).astype(BF16)
        o_ref[...] = jnp.dot(act, wd_ref[0], preferred_element_type=F32)

    @pl.when(pl.program_id(0) >= nv_ref[0])
    def _():
        o_ref[...] = jnp.zeros(o_ref.shape, o_ref.dtype)


def _experts(xs, g, tile_expert, n_valid, wgu, wd, tm):
    n, d = xs.shape
    d_ff = wd.shape[1]
    row_tile = lambda i, te, nv: (jnp.minimum(i, nv[0] - 1), 0)
    return pl.pallas_call(
        functools.partial(_expert_kernel, d_ff=d_ff),
        grid_spec=pltpu.PrefetchScalarGridSpec(
            num_scalar_prefetch=2,
            grid=(n // tm,),
            in_specs=[
                pl.BlockSpec((tm, d), row_tile),
                pl.BlockSpec((1, d), lambda i, te, nv: (0, 0)),
                pl.BlockSpec((1, d, 2 * d_ff), lambda i, te, nv: (te[i], 0, 0)),
                pl.BlockSpec((1, d_ff, d), lambda i, te, nv: (te[i], 0, 0)),
            ],
            out_specs=pl.BlockSpec((tm, d), lambda i, te, nv: (i, 0)),
        ),
        out_shape=jax.ShapeDtypeStruct((n, d), F32),
        compiler_params=_cparams(("arbitrary",)),
        name="moe_experts",
    )(tile_expert, n_valid, xs, g.reshape(1, d), wgu, wd)


def _combine_kernel(pos_ref, x_ref, gate_ref, y_hbm, o_ref, buf0, buf1, sem, *, rows):
    def start(t, carry):
        for u in range(ROW_UNROLL):
            r = t * ROW_UNROLL + u
            _row_copy(y_hbm, buf0, sem.at[0], pos_ref[0, 0, 2 * r], r).start(priority=0)
            _row_copy(y_hbm, buf1, sem.at[1], pos_ref[0, 0, 2 * r + 1], r).start(priority=1)
        return carry

    def wait(t, carry):
        for u in range(ROW_UNROLL):
            r = t * ROW_UNROLL + u
            _row_copy(y_hbm, buf0, sem.at[0], 0, r).wait()
            _row_copy(y_hbm, buf1, sem.at[1], 0, r).wait()
        return carry

    lax.fori_loop(0, rows // ROW_UNROLL, start, 0)
    lax.fori_loop(0, rows // ROW_UNROLL, wait, 0)
    o_ref[...] = x_ref[...] + (gate_ref[:, 0:1] * buf0[...] + gate_ref[:, 1:2] * buf1[...])


def _combine(x, ys, pos, gate, rows):
    m, d = x.shape
    return pl.pallas_call(
        functools.partial(_combine_kernel, rows=rows),
        grid=(m // rows,),
        in_specs=[pl.BlockSpec((1, 1, 2 * rows), lambda i: (i, 0, 0), memory_space=pltpu.SMEM),
                  pl.BlockSpec((rows, d), lambda i: (i, 0)), pl.BlockSpec((rows, LANES), lambda i: (i, 0)),
                  pl.BlockSpec(memory_space=pl.ANY)],
        out_specs=pl.BlockSpec((rows, d), lambda i: (i, 0)),
        scratch_shapes=[pltpu.VMEM((rows, d), F32), pltpu.VMEM((rows, d), F32), pltpu.SemaphoreType.DMA((2,))],
        out_shape=jax.ShapeDtypeStruct((m, d), F32),
        compiler_params=_cparams(("arbitrary",)),
        name="moe_combine",
    )(pos.reshape(m // rows, 1, 2 * rows), x, gate, ys)


def _moe(x, g, w_router, wgu, wd, tm):
    m = x.shape[0]
    idx, gate = _router(x, g, w_router)
    e_flat = idx[:, :2].reshape(-1)
    onehot = (e_flat[:, None] == jnp.arange(N_EXPERTS)[None, :]).astype(jnp.int32)
    counts = jnp.sum(onehot, axis=0)
    rank = jnp.sum((jnp.cumsum(onehot, axis=0) - onehot) * onehot, axis=1)
    padded = ((counts + tm - 1) // tm) * tm
    ends = jnp.cumsum(padded)
    starts = ends - padded
    pos = (jnp.sum(onehot * starts[None, :], axis=1) + rank).astype(jnp.int32)
    pad = jnp.concatenate([starts + counts, ends]).astype(jnp.int32)
    n_tiles = -(-(2 * m + N_EXPERTS * (tm - 1)) // tm)
    tile_start = jnp.arange(n_tiles, dtype=jnp.int32) * tm
    tile_expert = jnp.minimum(jnp.sum((tile_start[:, None] >= ends[None, :]).astype(jnp.int32), axis=1),
                              N_EXPERTS - 1).astype(jnp.int32)
    n_valid = (ends[-1] // tm).astype(jnp.int32).reshape(1)

    rows = min(m, 512)
    xs = _dispatch(x, pos, pad, n_tiles * tm, rows)
    ys = _experts(xs, g, tile_expert, n_valid, wgu, wd, tm)
    return _combine(x, ys, pos, gate, rows)


STEP_HB = 8


def _positions_last(cache):
    return cache.transpose(0, 1, 3, 4, 5, 2)


def _step_softmax(s, s_new, slope, sink, dil):
    length = s.shape[1]
    pos = lax.broadcasted_iota(jnp.int32, (1, length), 1)
    s = s - slope * (length - pos).astype(F32)
    if dil > 1:
        s = jnp.where((pos & (dil - 1)) == 0, s, MASKED)
    m = jnp.maximum(jnp.max(s, axis=1, keepdims=True), s_new)
    if sink is not None:
        m = jnp.maximum(m, sink)
    p = jnp.exp(s - m)
    p_new = jnp.exp(s_new - m)
    den = jnp.sum(p, axis=1, keepdims=True) + p_new
    if sink is not None:
        den = den + jnp.exp(sink - m)
    return p, p_new, 1.0 / den, m + jnp.log(den)


def _step_a_kernel(q_ref, c0_ref, c1_ref, c2_ref, slope_ref, o_ref):
    heads = range(STEP_HB)
    slope = slope_ref[...]
    outs, lses = [], []
    for g, (c_ref, (_, dil)) in enumerate(zip((c0_ref, c1_ref, c2_ref), A_GROUPS)):
        q = [q_ref[3 * g, :, i:i + 1] for i in heads]
        s = jnp.concatenate([jnp.sum(c_ref[0, i] * q[i], axis=0, keepdims=True) for i in heads], axis=0)
        s_new = jnp.concatenate([jnp.sum(q_ref[3 * g + 1, :, i:i + 1] * q[i], axis=0, keepdims=True)
                                 for i in heads], axis=0)
        p, p_new, inv, lse = _step_softmax(s, s_new, slope, None, dil)
        outs.append([(jnp.sum(c_ref[1, i] * p[i:i + 1, :], axis=1, keepdims=True)
                      + p_new[i:i + 1, :] * q_ref[3 * g + 2, :, i:i + 1]) * inv[i:i + 1, :] for i in heads])
        lses.append(lse)
    top = jnp.maximum(jnp.maximum(lses[0], lses[1]), lses[2])
    es = [jnp.exp(l - top) for l in lses]
    inv = 1.0 / (es[0] + es[1] + es[2])
    ws = [e * inv for e in es]
    o_ref[...] = jnp.concatenate(
        [sum(ws[g][i:i + 1, :] * outs[g][i] for g in range(len(A_GROUPS))) for i in heads], axis=1)


def _step_attention_a(qkv, caches, li):
    n = qkv.shape[0]
    hb = STEP_HB
    nhb = N_HEADS // hb
    n_seg = 3 * len(A_GROUPS)
    q_t = qkv.reshape(n, n_seg, nhb, hb, HEAD_DIM).transpose(0, 2, 1, 4, 3)
    slope = jnp.asarray(SLOPES, F32).reshape(nhb, hb, 1)
    cache_specs = [pl.BlockSpec((None, None, 2, hb, HEAD_DIM, c.shape[2]), lambda i, j: (li, i, 0, j, 0, 0))
                   for c in caches]
    out = pl.pallas_call(
        _step_a_kernel,
        grid=(n, nhb),
        in_specs=[pl.BlockSpec((None, None, n_seg, HEAD_DIM, hb), lambda i, j: (i, j, 0, 0, 0))] + cache_specs
        + [pl.BlockSpec((None, hb, 1), lambda i, j: (j, 0, 0))],
        out_specs=pl.BlockSpec((None, None, HEAD_DIM, hb), lambda i, j: (i, j, 0, 0)),
        out_shape=jax.ShapeDtypeStruct((n, nhb, HEAD_DIM, hb), F32),
        compiler_params=_cparams(("parallel", "parallel")),
        name="step_attn_a",
    )(q_t, *[_positions_last(c) for c in caches], slope)
    return out.transpose(0, 1, 3, 2).reshape(n, D_MODEL)


STEP_TB = 8


def _step_b_kernel(q_ref, kn_ref, vn_ref, c_ref, slope_ref, sink_ref, o_ref):
    rep = N_HEADS // HKV_B
    hi = lax.Precision.HIGHEST
    for t in range(q_ref.shape[0]):
        for kv in range(HKV_B):
            hs = slice(kv * rep, (kv + 1) * rep)
            q = q_ref[t, hs, :]
            s = jnp.dot(q, c_ref[t, 0, kv], preferred_element_type=F32, precision=hi)
            s_new = jnp.sum(q * kn_ref[t, kv:kv + 1, :], axis=1, keepdims=True)
            p, p_new, inv, _ = _step_softmax(s, s_new, slope_ref[hs, :], sink_ref[hs, :], 1)
            pv = lax.dot_general(p, c_ref[t, 1, kv], (((1,), (1,)), ((), ())), preferred_element_type=F32,
                                 precision=hi)
            o_ref[t, hs, :] = (pv + p_new * vn_ref[t, kv:kv + 1, :]) * inv


def _step_attention_b(qkv, cache, sink, li):
    n = qkv.shape[0]
    kvw = HKV_B * HEAD_DIM
    tb = min(n, STEP_TB)
    col = lambda v: v.astype(F32).reshape(N_HEADS, 1)
    tok = lambda h: pl.BlockSpec((tb, h, HEAD_DIM), lambda i: (i, 0, 0))
    out = pl.pallas_call(
        _step_b_kernel,
        grid=(n // tb,),
        in_specs=[tok(N_HEADS), tok(HKV_B), tok(HKV_B),
                  pl.BlockSpec((None, tb, 2, HKV_B, HEAD_DIM, cache.shape[2]), lambda i: (li, i, 0, 0, 0, 0)),
                  pl.BlockSpec((N_HEADS, 1), lambda i: (0, 0)), pl.BlockSpec((N_HEADS, 1), lambda i: (0, 0))],
        out_specs=tok(N_HEADS),
        out_shape=jax.ShapeDtypeStruct((n, N_HEADS, HEAD_DIM), F32),
        compiler_params=_cparams(("parallel",)),
        name="step_attn_b",
    )(qkv[:, :D_MODEL].reshape(n, N_HEADS, HEAD_DIM), qkv[:, D_MODEL:D_MODEL + kvw].reshape(n, HKV_B, HEAD_DIM),
      qkv[:, D_MODEL + kvw:].reshape(n, HKV_B, HEAD_DIM), _positions_last(cache), col(jnp.asarray(SLOPES, F32)),
      col(sink))
    return out.reshape(n, D_MODEL)


def _tile_heads(v):
    return jnp.tile(v.astype(F32), N_HEADS)


def _kv_tail_kernel(x_ref, g_ref, wk_ref, wv_ref, gain_ref, seg_ref, o_ref, v_scr):
    h = _rms(x_ref[...], g_ref[...]).astype(BF16)
    k = jnp.dot(h, wk_ref[...], preferred_element_type=F32)
    v_scr[...] = jnp.dot(h, wv_ref[...], preferred_element_type=F32)
    width = k.shape[1]
    sq = (k * k).astype(BF16)
    wc = seg_ref.shape[0]
    parts = [jnp.dot(sq[:, c0:c0 + wc], seg_ref[...], preferred_element_type=F32) for c0 in range(0, width, wc)]
    ssq = parts[0] if len(parts) == 1 else jnp.concatenate(parts, axis=1)
    k = k * lax.rsqrt(ssq * (1.0 / HEAD_DIM) + RMS_EPS) * gain_ref[...]
    o_ref[0, :width, :] = k.T
    o_ref[0, width:, :] = v_scr[...].T


def _kv_tail(x, g, w, gain, *, batch, keep, k_blk, width, heads):
    m, d = x.shape
    s_len = m // batch
    tk = min(keep, 512)
    first = (s_len - keep) // tk
    wseg = min(width, MXU_DIM)
    seg = jnp.asarray(np.kron(np.eye(wseg // HEAD_DIM), np.ones((HEAD_DIM, HEAD_DIM))), BF16)
    out = pl.pallas_call(
        _kv_tail_kernel,
        grid=(batch, keep // tk),
        in_specs=[
            pl.BlockSpec((tk, d), lambda b, j: (b * (s_len // tk) + first + j, 0)),
            pl.BlockSpec((1, d), lambda b, j: (0, 0)),
            pl.BlockSpec((d, width), lambda b, j: (0, k_blk)),
            pl.BlockSpec((d, width), lambda b, j: (0, k_blk + 1)),
            pl.BlockSpec((1, width), lambda b, j: (0, k_blk)),
            pl.BlockSpec((wseg, wseg), lambda b, j: (0, 0)),
        ],
        out_specs=pl.BlockSpec((1, 2 * width, tk), lambda b, j: (b, 0, j)),
        out_shape=jax.ShapeDtypeStruct((batch, 2 * width, keep), F32),
        scratch_shapes=[pltpu.VMEM((tk, width), F32)],
        compiler_params=_cparams(("parallel", "parallel")),
        name="kv_tail",
    )(x, g.reshape(1, d), w, w, gain, seg)
    return out.reshape(batch, 2, heads, HEAD_DIM, keep).transpose(0, 4, 1, 2, 3)


def kernel(x_prompt, x_sample, cache_a_w128, cache_a_w512, cache_a_w2048, cache_b, norm_mix_a, w_in_a, q_gain_a, k_gain_a, w_out_a, norm_ffn_dense, w_gu_dense, w_down_dense, norm_mix_b, w_in_b, q_gain_b, k_gain_b, sink_b, w_out_b, norm_ffn_moe, w_router, w_gu_moe, w_down_moe):
    batch, s_len, d = x_prompt.shape
    n_dec = x_sample.shape[0]
    assert x_sample.shape[1] == 1 and d == D_MODEL
    caches_a = (cache_a_w128, cache_a_w512, cache_a_w2048)
    xp = x_prompt.reshape(batch * s_len, d)
    xs = x_sample.reshape(n_dec, d)
    q_scale = HEAD_DIM ** -0.5
    n_grp = len(A_GROUPS)
    outs = []

    li = 0
    w_in = w_in_a[li].astype(BF16)
    cols_a = w_in.shape[1]
    gain = jnp.concatenate([jnp.concatenate([_tile_heads(q_gain_a[li, g]) * q_scale, _tile_heads(k_gain_a[li, g]),
                                             jnp.ones((d,), F32)]) for g in range(n_grp)]).reshape(1, cols_a)
    flag = jnp.tile(jnp.concatenate([jnp.ones((2 * d,), F32), jnp.zeros((d,), F32)]), n_grp).reshape(1, cols_a)
    qkv_s = _proj(xs, norm_mix_a[li], w_in, gain, flag, batch=1, dil=1, tn=d, col0=0, ncols=cols_a)
    qkv_s = qkv_s.reshape(n_dec, cols_a)
    no_sink = jnp.zeros((N_HEADS,), F32)
    o_p, l_p = [], []
    for g, (window, dil) in enumerate(A_GROUPS):
        qkv_g = _proj(xp, norm_mix_a[li], w_in, gain, flag, batch=batch, dil=dil, tn=d, col0=3 * g, ncols=3 * d)
        o, l = _band_attention(qkv_g, no_sink, q_blk=0, k_blk=1, v_blk=2, kv_heads=N_HEADS, has_sink=False)
        o_p.append(o)
        l_p.append(l)
        outs.append(_kv_tail(xp, norm_mix_a[li], w_in, gain, batch=batch, keep=min(window, s_len), k_blk=3 * g + 1,
                             width=d, heads=N_HEADS)[None])
        kv_s = qkv_s[:, (3 * g + 1) * d:(3 * g + 3) * d]
        outs.append(kv_s.reshape(1, n_dec, 1, 2, N_HEADS, HEAD_DIM).astype(F32))
    w_out = w_out_a[li].astype(BF16)
    xp = _merge_out(o_p, l_p, w_out, xp)
    xs = _out_proj(_step_attention_a(qkv_s.astype(F32), caches_a, li), w_out, xs)
    wgu, wd = w_gu_dense[li].astype(BF16), w_down_dense[li].astype(BF16)
    xp = _ffn_dense(xp, norm_ffn_dense[li], wgu, wd)
    xs = _ffn_dense(xs, norm_ffn_dense[li], wgu, wd)

    w_in = w_in_b[li].astype(BF16)
    cols_b = w_in.shape[1]
    kvw = HKV_B * HEAD_DIM
    gain = jnp.concatenate([_tile_heads(q_gain_b[li]) * q_scale, jnp.tile(k_gain_b[li].astype(F32), HKV_B),
                            jnp.ones((kvw,), F32)]).reshape(1, cols_b)
    flag = jnp.concatenate([jnp.ones((d + kvw,), F32), jnp.zeros((kvw,), F32)]).reshape(1, cols_b)
    qkv_p = _proj(xp, norm_mix_b[li], w_in, gain, flag, batch=batch, dil=1, tn=cols_b, col0=0, ncols=cols_b)
    qkv_s = _proj(xs, norm_mix_b[li], w_in, gain, flag, batch=1, dil=1, tn=cols_b, col0=0, ncols=cols_b)
    qkv_s = qkv_s.reshape(n_dec, cols_b).astype(F32)
    sink = sink_b[li].astype(F32)
    o, _ = _band_attention(qkv_p, sink, q_blk=0, k_blk=d // kvw, v_blk=d // kvw + 1, kv_heads=HKV_B, has_sink=True)
    outs.append(_kv_tail(xp, norm_mix_b[li], w_in, gain, batch=batch, keep=min(N_STEPS, s_len), k_blk=d // kvw,
                         width=kvw, heads=HKV_B)[None])
    outs.append(qkv_s[:, d:].reshape(1, n_dec, 1, 2, HKV_B, HEAD_DIM))
    w_out = w_out_b[li].astype(BF16)
    xp = _out_proj(o.reshape(batch * s_len, d), w_out, xp)
    xs = _out_proj(_step_attention_b(qkv_s, cache_b, sink, li), w_out, xs)
    wgu, wd = w_gu_moe[li].astype(BF16), w_down_moe[li].astype(BF16)
    xp = _moe(xp, norm_ffn_moe[li], w_router[li], wgu, wd, 512)
    xs = _moe(xs, norm_ffn_moe[li], w_router[li], wgu, wd, 128)

    return (xp.reshape(batch, s_len, d), xs.reshape(n_dec, 1, d), *outs)
```

```python
import functools

import numpy as np
import jax
import jax.numpy as jnp
from jax import lax
from jax.experimental import pallas as pl
from jax.experimental.pallas import tpu as pltpu

F32 = jnp.float32
BF16 = jnp.bfloat16

D_MODEL = 1024
HEAD_DIM = 64
N_HEADS = 16
HKV_B = 2
A_GROUPS = ((128, 1), (512, 4), (2048, 16))
N_STEPS = 128
BLK = 128
N_EXPERTS = 8
RMS_EPS = 1e-6
MASKED = -1e30
LANES = 128
SUBLANES = 8
MXU_DIM = 256
VMEM_LIMIT = 56 * 1024 * 1024
D_TILES = D_MODEL // LANES

SLOPES = tuple(float(2.0 ** (-8.0 * (h + 1) / N_HEADS)) for h in range(N_HEADS))


def _cparams(sem):
    return pltpu.CompilerParams(dimension_semantics=sem, vmem_limit_bytes=VMEM_LIMIT)


def _rms(x, g):
    ms = jnp.mean(x * x, axis=-1, keepdims=True)
    return x * lax.rsqrt(ms + RMS_EPS) * g


def _proj_kernel(x_ref, g_ref, w_ref, gain_ref, flag_ref, seg_ref, o_ref, h_scr, *perm_scr, dil):
    @pl.when(pl.program_id(1) == 0)
    def _():
        h = _rms(x_ref[...], g_ref[...])
        if dil == 1:
            h_scr[...] = h.astype(BF16)
        else:
            scr = perm_scr[0]
            n = h.shape[0] // dil
            for c in range(h.shape[1] // LANES):
                cols = slice(c * LANES, (c + 1) * LANES)
                scr[c] = h[:, cols]
                for r in range(dil):
                    h_scr[r * n:(r + 1) * n, cols] = scr[c, pl.ds(r, n, stride=dil), :].astype(BF16)

    acc = jnp.dot(h_scr[...], w_ref[...], preferred_element_type=F32)
    sq = (acc * acc).astype(BF16)
    tm, tn = acc.shape
    ssq = jnp.concatenate(
        [jnp.dot(sq[:, c * MXU_DIM:(c + 1) * MXU_DIM], seg_ref[...], preferred_element_type=F32)
         for c in range(tn // MXU_DIM)], axis=1)
    nrm = acc * lax.rsqrt(ssq * (1.0 / HEAD_DIM) + RMS_EPS) * gain_ref[...]
    res = jnp.where(flag_ref[...] > 0.0, nrm, acc)
    for r in range(dil):
        o_ref[0, r] = res[r * (tm // dil):(r + 1) * (tm // dil)].astype(o_ref.dtype)


def _proj(x, g, w, gain, flag, *, batch, dil, tn, col0, ncols):
    m, d = x.shape
    s_len = m // batch
    tm = min(s_len, 1024)
    tpb = s_len // tm
    seg = jnp.asarray(np.kron(np.eye(MXU_DIM // HEAD_DIM), np.ones((HEAD_DIM, HEAD_DIM))), BF16)
    scratch = [pltpu.VMEM((tm, d), BF16)] + ([pltpu.VMEM((d // LANES, tm, LANES), F32)] if dil > 1 else [])
    return pl.pallas_call(
        functools.partial(_proj_kernel, dil=dil),
        grid=(m // tm, ncols // tn),
        in_specs=[
            pl.BlockSpec((tm, d), lambda i, j: (i, 0)),
            pl.BlockSpec((1, d), lambda i, j: (0, 0)),
            pl.BlockSpec((d, tn), lambda i, j: (0, col0 + j)),
            pl.BlockSpec((1, tn), lambda i, j: (0, col0 + j)),
            pl.BlockSpec((1, tn), lambda i, j: (0, col0 + j)),
            pl.BlockSpec((MXU_DIM, MXU_DIM), lambda i, j: (0, 0)),
        ],
        out_specs=pl.BlockSpec((1, dil, tm // dil, tn), lambda i, j: (i // tpb, 0, i % tpb, j)),
        out_shape=jax.ShapeDtypeStruct((batch, dil, s_len // dil, ncols), BF16),
        scratch_shapes=scratch,
        compiler_params=_cparams(("parallel", "arbitrary")),
        name="proj_qknorm_d%d" % dil,
    )(x, g.reshape(1, d), w, gain, flag, seg)


def _band_kernel(sink_ref, q_ref, kp_ref, kc_ref, vp_ref, vc_ref, o_ref, lse_ref, k_all, v_all, *, dil, qb,
                 kv_heads, has_sink):
    c = pl.program_id(2)
    k_all[0:BLK] = kp_ref[0, 0]
    k_all[BLK:] = kc_ref[0, 0]
    v_all[0:BLK] = vp_ref[0, 0]
    v_all[BLK:] = vc_ref[0, 0]

    qi = lax.broadcasted_iota(jnp.int32, (BLK, 2 * BLK), 0)
    sj = lax.broadcasted_iota(jnp.int32, (BLK, 2 * BLK), 1)
    dist = qi - sj + BLK
    base = jnp.where((dist >= 0) & (dist <= N_STEPS), -(dist.astype(F32) * float(dil)), MASKED)
    base_first = jnp.where(sj >= BLK, base, MASKED)
    low_k = lax.broadcasted_iota(jnp.int32, (2 * BLK, LANES), 1) < HEAD_DIM
    lane_q = lax.broadcasted_iota(jnp.int32, (BLK, LANES), 1)
    low_q = lane_q < HEAD_DIM
    zeros_k = jnp.zeros((2 * BLK, LANES), BF16)
    ones_a = jnp.where(low_k, 1.0, 0.0).astype(BF16)
    ones_b = jnp.where(low_k, 0.0, 1.0).astype(BF16)
    nt = (((1,), (1,)), ((), ()))

    for i in range(qb):
        rows = slice(i * BLK, (i + 1) * BLK)
        win = slice(i * BLK, (i + 2) * BLK)
        base_i = jnp.where(c == 0, base_first, base) if i == 0 else base
        lse_acc = jnp.zeros((BLK, LANES), F32)
        if kv_heads != N_HEADS:
            kw, vw = k_all[win, :], v_all[win, :]
            zero_half = jnp.zeros((2 * BLK, HEAD_DIM), BF16)
            shared = {}
            for kv in range(kv_heads):
                k_kv = kw[:, kv * HEAD_DIM:(kv + 1) * HEAD_DIM]
                v_kv = vw[:, kv * HEAD_DIM:(kv + 1) * HEAD_DIM]
                shared[kv] = tuple((jnp.concatenate([t, zero_half], axis=1), jnp.concatenate([zero_half, t], axis=1))
                                   for t in (k_kv, v_kv))
        for pair in range(N_HEADS // 2):
            pc = slice(pair * LANES, (pair + 1) * LANES)
            ha, hb = 2 * pair, 2 * pair + 1
            qp = q_ref[0, 0, rows, pc]
            if kv_heads == N_HEADS:
                kw, vw = k_all[win, pc], v_all[win, pc]
                k_a, k_b = jnp.where(low_k, kw, zeros_k), jnp.where(low_k, zeros_k, kw)
                v_a, v_b = jnp.where(low_k, vw, zeros_k), jnp.where(low_k, zeros_k, vw)
            else:
                (k_a, k_b), (v_a, v_b) = shared[ha // (N_HEADS // kv_heads)]
            s_a = lax.dot_general(qp, k_a, nt, preferred_element_type=F32) + SLOPES[ha] * base_i
            s_b = lax.dot_general(qp, k_b, nt, preferred_element_type=F32) + SLOPES[hb] * base_i
            m_a = jnp.max(s_a, axis=-1, keepdims=True)
            m_b = jnp.max(s_b, axis=-1, keepdims=True)
            if has_sink:
                m_a = jnp.maximum(m_a, sink_ref[ha])
                m_b = jnp.maximum(m_b, sink_ref[hb])
            p_a = jnp.exp(s_a - m_a).astype(BF16)
            p_b = jnp.exp(s_b - m_b).astype(BF16)
            res = (jnp.dot(p_a, jnp.concatenate([v_a, ones_a], axis=1), preferred_element_type=F32)
                   + jnp.dot(p_b, jnp.concatenate([v_b, ones_b], axis=1), preferred_element_type=F32))
            den = res[:, LANES:]
            m_pair = jnp.where(low_q, m_a, m_b)
            if has_sink:
                den = den + jnp.where(low_q, jnp.exp(sink_ref[ha] - m_a), jnp.exp(sink_ref[hb] - m_b))
            o_ref[0, 0, rows, pc] = (res[:, :LANES] * (1.0 / den)).astype(o_ref.dtype)
            lse_acc = jnp.where((lane_q & (HEAD_DIM - 1)) == pair, m_pair + jnp.log(den), lse_acc)
        lse_ref[0, 0, rows, :] = lse_acc


def _band_attention(qkv, sink, *, q_blk, k_blk, v_blk, kv_heads, has_sink):
    batch, dil, ls, _ = qkv.shape
    qb = min(4, ls // BLK)
    rows = BLK * qb
    kvw = kv_heads * HEAD_DIM
    kern = functools.partial(_band_kernel, dil=dil, qb=qb, kv_heads=kv_heads, has_sink=has_sink)
    prev = lambda b, r, c: jnp.maximum(c * qb - 1, 0)
    return pl.pallas_call(
        kern,
        grid=(batch, dil, ls // rows),
        in_specs=[
            pl.BlockSpec(memory_space=pltpu.SMEM),
            pl.BlockSpec((1, 1, rows, D_MODEL), lambda b, r, c: (b, r, c, q_blk)),
            pl.BlockSpec((1, 1, BLK, kvw), lambda b, r, c: (b, r, prev(b, r, c), k_blk)),
            pl.BlockSpec((1, 1, rows, kvw), lambda b, r, c: (b, r, c, k_blk)),
            pl.BlockSpec((1, 1, BLK, kvw), lambda b, r, c: (b, r, prev(b, r, c), v_blk)),
            pl.BlockSpec((1, 1, rows, kvw), lambda b, r, c: (b, r, c, v_blk)),
        ],
        out_specs=[pl.BlockSpec((1, 1, rows, D_MODEL), lambda b, r, c: (b, r, c, 0)),
                   pl.BlockSpec((1, 1, rows, LANES), lambda b, r, c: (b, r, c, 0))],
        out_shape=[jax.ShapeDtypeStruct((batch, dil, ls, D_MODEL), BF16),
                   jax.ShapeDtypeStruct((batch, dil, ls, LANES), F32)],
        scratch_shapes=[pltpu.VMEM((BLK + rows, kvw), BF16), pltpu.VMEM((BLK + rows, kvw), BF16)],
        compiler_params=_cparams(("parallel", "parallel", "arbitrary")),
        name="band_attn_d%d" % dil,
    )(sink, qkv, qkv, qkv, qkv, qkv)


def _lse_lane(h):
    return h // 2 + HEAD_DIM * (h % 2)


def _head_expand():
    e = np.zeros((LANES, D_MODEL), np.float32)
    for h in range(N_HEADS):
        e[_lse_lane(h), h * HEAD_DIM:(h + 1) * HEAD_DIM] = 1.0
    return jnp.asarray(e, BF16)


def _split_dot(a, b):
    hi = a.astype(BF16)
    lo = (a - hi.astype(F32)).astype(BF16)
    return jnp.dot(hi, b, preferred_element_type=F32) + jnp.dot(lo, b, preferred_element_type=F32)


def _merge_out_kernel(*refs, dils):
    o_refs, l_refs = refs[0:3], refs[3:6]
    e_ref, w_ref, x_ref, out_ref = refs[6:10]
    scr = refs[10:]
    tm = x_ref.shape[0]
    outs, lses = [], []
    k = 0
    for o_ref, l_ref, dil in zip(o_refs, l_refs, dils):
        if dil == 1:
            outs.append(o_ref[0, 0].astype(F32))
            lses.append(l_ref[0, 0])
        else:
            so, sl = scr[k], scr[k + 1]
            k += 2
            for r in range(dil):
                rows = pl.ds(r, tm // dil, stride=dil)
                sl[rows, :] = l_ref[0, r]
                for c in range(D_TILES):
                    so[c, rows, :] = o_ref[0, r, :, c * LANES:(c + 1) * LANES].astype(F32)
            outs.append(jnp.concatenate([so[c] for c in range(D_TILES)], axis=1))
            lses.append(sl[...])
    top = jnp.maximum(jnp.maximum(lses[0], lses[1]), lses[2])
    es = [jnp.exp(l - top) for l in lses]
    inv = 1.0 / (es[0] + es[1] + es[2])
    mix = sum(_split_dot(e * inv, e_ref[...]) * o for e, o in zip(es, outs))
    out_ref[...] = x_ref[...] + jnp.dot(mix.astype(BF16), w_ref[...], preferred_element_type=F32)


def _merge_out(outs, lses, w, x):
    m, d = x.shape
    batch = outs[0].shape[0]
    s_len = m // batch
    tm = min(s_len, 256)
    tpb = s_len // tm
    dils = tuple(o.shape[1] for o in outs)
    grp = lambda width: [pl.BlockSpec((1, dil, tm // dil, width), lambda i: (i // tpb, 0, i % tpb, 0))
                         for dil in dils]
    scratch = [s for dil in dils if dil > 1
               for s in (pltpu.VMEM((D_TILES, tm, LANES), F32), pltpu.VMEM((tm, LANES), F32))]
    return pl.pallas_call(
        functools.partial(_merge_out_kernel, dils=dils),
        grid=(m // tm,),
        in_specs=grp(d) + grp(LANES) + [pl.BlockSpec((LANES, d), lambda i: (0, 0)),
                                         pl.BlockSpec((d, d), lambda i: (0, 0)),
                                         pl.BlockSpec((tm, d), lambda i: (i, 0))],
        out_specs=pl.BlockSpec((tm, d), lambda i: (i, 0)),
        out_shape=jax.ShapeDtypeStruct((m, d), F32),
        scratch_shapes=scratch,
        compiler_params=_cparams(("parallel",)),
        name="merge_outproj",
    )(*outs, *lses, _head_expand(), w, x)


def _out_kernel(o_ref, w_ref, x_ref, out_ref):
    out_ref[...] = x_ref[...] + jnp.dot(o_ref[...].astype(BF16), w_ref[...], preferred_element_type=F32)


def _out_proj(o, w, x):
    m, d = x.shape
    tm = min(m, 1024)
    return pl.pallas_call(
        _out_kernel,
        grid=(m // tm,),
        in_specs=[pl.BlockSpec((tm, d), lambda i: (i, 0)), pl.BlockSpec((d, d), lambda i: (0, 0)),
                  pl.BlockSpec((tm, d), lambda i: (i, 0))],
        out_specs=pl.BlockSpec((tm, d), lambda i: (i, 0)),
        out_shape=jax.ShapeDtypeStruct((m, d), F32),
        compiler_params=_cparams(("parallel",)),
        name="outproj",
    )(o, w, x)


def _silu(g):
    return g * (1.0 / (1.0 + jnp.exp(-g)))


def _ffn_kernel(x_ref, g_ref, wgu_ref, wd_ref, out_ref, *, d_ff, chunk):
    x = x_ref[...]
    h = _rms(x, g_ref[...]).astype(BF16)
    acc = jnp.zeros(x.shape, F32)
    for c in range(d_ff // chunk):
        gate = jnp.dot(h, wgu_ref[:, c * chunk:(c + 1) * chunk], preferred_element_type=F32)
        up = jnp.dot(h, wgu_ref[:, d_ff + c * chunk:d_ff + (c + 1) * chunk], preferred_element_type=F32)
        act = (_silu(gate) * up).astype(BF16)
        acc = acc + jnp.dot(act, wd_ref[c * chunk:(c + 1) * chunk, :], preferred_element_type=F32)
    out_ref[...] = x + acc


def _ffn_dense(x, g, wgu, wd):
    m, d = x.shape
    tm = min(m, 512)
    d_ff = wd.shape[0]
    kern = functools.partial(_ffn_kernel, d_ff=d_ff, chunk=d_ff // 2)
    return pl.pallas_call(
        kern,
        grid=(m // tm,),
        in_specs=[pl.BlockSpec((tm, d), lambda i: (i, 0)), pl.BlockSpec((1, d), lambda i: (0, 0)),
                  pl.BlockSpec((d, 2 * d_ff), lambda i: (0, 0), pipeline_mode=pl.Buffered(1)),
                  pl.BlockSpec((d_ff, d), lambda i: (0, 0), pipeline_mode=pl.Buffered(1))],
        out_specs=pl.BlockSpec((tm, d), lambda i: (i, 0)),
        out_shape=jax.ShapeDtypeStruct((m, d), F32),
        compiler_params=_cparams(("parallel",)),
        name="ffn_dense",
    )(x, g.reshape(1, d), wgu, wd)


def _router_kernel(x_ref, g_ref, wr_ref, idx_ref, gate_ref):
    h = _rms(x_ref[...], g_ref[...])
    logits = jnp.dot(h, wr_ref[...], preferred_element_type=F32, precision=lax.Precision.HIGHEST)
    lane = lax.broadcasted_iota(jnp.int32, logits.shape, 1)
    logits = jnp.where(lane < N_EXPERTS, logits, -jnp.inf)
    t1 = jnp.max(logits, axis=-1, keepdims=True)
    i1 = jnp.min(jnp.where(logits == t1, lane, LANES), axis=-1, keepdims=True)
    rest = jnp.where(lane == i1, -jnp.inf, logits)
    t2 = jnp.max(rest, axis=-1, keepdims=True)
    i2 = jnp.min(jnp.where(rest == t2, lane, LANES), axis=-1, keepdims=True)
    e2 = jnp.exp(t2 - t1)
    g1 = 1.0 / (1.0 + e2)
    g2 = e2 / (1.0 + e2)
    idx_ref[...] = jnp.where(lane == 0, i1, jnp.where(lane == 1, i2, 0))
    gate_ref[...] = jnp.where(lane == 0, g1, jnp.where(lane == 1, g2, 0.0))


def _router(x, g, w_router):
    m, d = x.shape
    tm = min(m, 1024)
    wr = jnp.zeros((d, LANES), F32).at[:, :N_EXPERTS].set(w_router)
    return pl.pallas_call(
        _router_kernel,
        grid=(m // tm,),
        in_specs=[pl.BlockSpec((tm, d), lambda i: (i, 0)), pl.BlockSpec((1, d), lambda i: (0, 0)),
                  pl.BlockSpec((d, LANES), lambda i: (0, 0))],
        out_specs=[pl.BlockSpec((tm, LANES), lambda i: (i, 0))] * 2,
        out_shape=[jax.ShapeDtypeStruct((m, LANES), jnp.int32), jax.ShapeDtypeStruct((m, LANES), F32)],
        compiler_params=_cparams(("parallel",)),
        name="router_top2",
    )(x, g.reshape(1, d), wr)


ROW_UNROLL = 4


def _row_copy(src, dst, sem, src_row, dst_row):
    return pltpu.make_async_copy(src.at[pl.ds(src_row, 1)], dst.at[pl.ds(dst_row, 1)], sem)


ZERO_ROWS = 128


def _dispatch_kernel(pad_ref, pos_ref, x_ref, xs_hbm, zeros, sem, *, rows):
    @pl.when(pl.program_id(0) == 0)
    def _():
        zeros[...] = jnp.zeros(zeros.shape, zeros.dtype)

        def pad_start(r, carry):
            _row_copy(zeros, xs_hbm, sem, 0, r).start()
            return carry

        def pad_wait(r, carry):
            _row_copy(zeros, xs_hbm, sem, 0, r).wait()
            return carry

        for e in range(N_EXPERTS):
            lax.fori_loop(pad_ref[e], pad_ref[N_EXPERTS + e], pad_start, 0)
            lax.fori_loop(pad_ref[e], pad_ref[N_EXPERTS + e], pad_wait, 0)

        def tail_copy(b):
            return pltpu.make_async_copy(zeros, xs_hbm.at[pl.ds(b * ZERO_ROWS, ZERO_ROWS)], sem)

        def tail_start(b, carry):
            tail_copy(b).start()
            return carry

        def tail_wait(b, carry):
            tail_copy(b).wait()
            return carry

        first, last = pad_ref[2 * N_EXPERTS - 1] // ZERO_ROWS, xs_hbm.shape[0] // ZERO_ROWS
        lax.fori_loop(first, last, tail_start, 0)
        lax.fori_loop(first, last, tail_wait, 0)

    def start(t, carry):
        for u in range(ROW_UNROLL):
            r = t * ROW_UNROLL + u
            _row_copy(x_ref, xs_hbm, sem, r, pos_ref[0, 0, 2 * r]).start(priority=0)
            _row_copy(x_ref, xs_hbm, sem, r, pos_ref[0, 0, 2 * r + 1]).start(priority=1)
        return carry

    def wait(t, carry):
        for u in range(2 * ROW_UNROLL):
            _row_copy(x_ref, xs_hbm, sem, 0, 0).wait()
        return carry

    lax.fori_loop(0, rows // ROW_UNROLL, start, 0)
    lax.fori_loop(0, rows // ROW_UNROLL, wait, 0)


def _dispatch(x, pos, pad, n_rows, rows):
    m, d = x.shape
    assert n_rows % ZERO_ROWS == 0
    return pl.pallas_call(
        functools.partial(_dispatch_kernel, rows=rows),
        grid_spec=pltpu.PrefetchScalarGridSpec(
            num_scalar_prefetch=1,
            grid=(m // rows,),
            in_specs=[pl.BlockSpec((1, 1, 2 * rows), lambda i, pad_ref: (i, 0, 0), memory_space=pltpu.SMEM),
                      pl.BlockSpec((rows, d), lambda i, pad_ref: (i, 0))],
            out_specs=pl.BlockSpec(memory_space=pl.ANY),
            scratch_shapes=[pltpu.VMEM((ZERO_ROWS, d), F32), pltpu.SemaphoreType.DMA(())],
        ),
        out_shape=jax.ShapeDtypeStruct((n_rows, d), x.dtype),
        compiler_params=_cparams(("arbitrary",)),
        name="moe_dispatch",
    )(pad, pos.reshape(m // rows, 1, 2 * rows), x)


def _expert_kernel(te_ref, nv_ref, xs_ref, g_ref, wgu_ref, wd_ref, o_ref, *, d_ff):
    @pl.when(pl.program_id(0) < nv_ref[0])
    def _():
        h = _rms(xs_ref[...], g_ref[...]).astype(BF16)
        gate = jnp.dot(h, wgu_ref[0, :, :d_ff], preferred_element_type=F32)
        up = jnp.dot(h, wgu_ref[0, :, d_ff:], preferred_element_type=F32)
        act = (_silu(gate) * up).astype(BF16)
        o_ref[...] = jnp.dot(act, wd_ref[0], preferred_element_type=F32)

    @pl.when(pl.program_id(0) >= nv_ref[0])
    def _():
        o_ref[...] = jnp.zeros(o_ref.shape, o_ref.dtype)


def _experts(xs, g, tile_expert, n_valid, wgu, wd, tm):
    n, d = xs.shape
    d_ff = wd.shape[1]
    row_tile = lambda i, te, nv: (jnp.minimum(i, nv[0] - 1), 0)
    return pl.pallas_call(
        functools.partial(_expert_kernel, d_ff=d_ff),
        grid_spec=pltpu.PrefetchScalarGridSpec(
            num_scalar_prefetch=2,
            grid=(n // tm,),
            in_specs=[
                pl.BlockSpec((tm, d), row_tile),
                pl.BlockSpec((1, d), lambda i, te, nv: (0, 0)),
                pl.BlockSpec((1, d, 2 * d_ff), lambda i, te, nv: (te[i], 0, 0)),
                pl.BlockSpec((1, d_ff, d), lambda i, te, nv: (te[i], 0, 0)),
            ],
            out_specs=pl.BlockSpec((tm, d), lambda i, te, nv: (i, 0)),
        ),
        out_shape=jax.ShapeDtypeStruct((n, d), F32),
        compiler_params=_cparams(("arbitrary",)),
        name="moe_experts",
    )(tile_expert, n_valid, xs, g.reshape(1, d), wgu, wd)


def _combine_kernel(pos_ref, x_ref, gate_ref, y_hbm, o_ref, buf0, buf1, sem, *, rows):
    def start(t, carry):
        for u in range(ROW_UNROLL):
            r = t * ROW_UNROLL + u
            _row_copy(y_hbm, buf0, sem.at[0], pos_ref[0, 0, 2 * r], r).start(priority=0)
            _row_copy(y_hbm, buf1, sem.at[1], pos_ref[0, 0, 2 * r + 1], r).start(priority=1)
        return carry

    def wait(t, carry):
        for u in range(ROW_UNROLL):
            r = t * ROW_UNROLL + u
            _row_copy(y_hbm, buf0, sem.at[0], 0, r).wait()
            _row_copy(y_hbm, buf1, sem.at[1], 0, r).wait()
        return carry

    lax.fori_loop(0, rows // ROW_UNROLL, start, 0)
    lax.fori_loop(0, rows // ROW_UNROLL, wait, 0)
    o_ref[...] = x_ref[...] + (gate_ref[:, 0:1] * buf0[...] + gate_ref[:, 1:2] * buf1[...])


def _combine(x, ys, pos, gate, rows):
    m, d = x.shape
    return pl.pallas_call(
        functools.partial(_combine_kernel, rows=rows),
        grid=(m // rows,),
        in_specs=[pl.BlockSpec((1, 1, 2 * rows), lambda i: (i, 0, 0), memory_space=pltpu.SMEM),
                  pl.BlockSpec((rows, d), lambda i: (i, 0)), pl.BlockSpec((rows, LANES), lambda i: (i, 0)),
                  pl.BlockSpec(memory_space=pl.ANY)],
        out_specs=pl.BlockSpec((rows, d), lambda i: (i, 0)),
        scratch_shapes=[pltpu.VMEM((rows, d), F32), pltpu.VMEM((rows, d), F32), pltpu.SemaphoreType.DMA((2,))],
        out_shape=jax.ShapeDtypeStruct((m, d), F32),
        compiler_params=_cparams(("arbitrary",)),
        name="moe_combine",
    )(pos.reshape(m // rows, 1, 2 * rows), x, gate, ys)


def _moe(x, g, w_router, wgu, wd, tm):
    m = x.shape[0]
    idx, gate = _router(x, g, w_router)
    e_flat = idx[:, :2].reshape(-1)
    onehot = (e_flat[:, None] == jnp.arange(N_EXPERTS)[None, :]).astype(jnp.int32)
    counts = jnp.sum(onehot, axis=0)
    rank = jnp.sum((jnp.cumsum(onehot, axis=0) - onehot) * onehot, axis=1)
    padded = ((counts + tm - 1) // tm) * tm
    ends = jnp.cumsum(padded)
    starts = ends - padded
    pos = (jnp.sum(onehot * starts[None, :], axis=1) + rank).astype(jnp.int32)
    pad = jnp.concatenate([starts + counts, ends]).astype(jnp.int32)
    n_tiles = -(-(2 * m + N_EXPERTS * (tm - 1)) // tm)
    tile_start = jnp.arange(n_tiles, dtype=jnp.int32) * tm
    tile_expert = jnp.minimum(jnp.sum((tile_start[:, None] >= ends[None, :]).astype(jnp.int32), axis=1),
                              N_EXPERTS - 1).astype(jnp.int32)
    n_valid = (ends[-1] // tm).astype(jnp.int32).reshape(1)

    rows = min(m, 512)
    xs = _dispatch(x, pos, pad, n_tiles * tm, rows)
    ys = _experts(xs, g, tile_expert, n_valid, wgu, wd, tm)
    return _combine(x, ys, pos, gate, rows)


STEP_HB = 8


def _positions_last(cache):
    return cache.transpose(0, 1, 3, 4, 5, 2)


def _step_softmax(s, s_new, slope, sink, dil):
    length = s.shape[1]
    pos = lax.broadcasted_iota(jnp.int32, (1, length), 1)
    s = s - slope * (length - pos).astype(F32)
    if dil > 1:
        s = jnp.where((pos & (dil - 1)) == 0, s, MASKED)
    m = jnp.maximum(jnp.max(s, axis=1, keepdims=True), s_new)
    if sink is not None:
        m = jnp.maximum(m, sink)
    p = jnp.exp(s - m)
    p_new = jnp.exp(s_new - m)
    den = jnp.sum(p, axis=1, keepdims=True) + p_new
    if sink is not None:
        den = den + jnp.exp(sink - m)
    return p, p_new, 1.0 / den, m + jnp.log(den)


def _step_a_kernel(q_ref, c0_ref, c1_ref, c2_ref, slope_ref, o_ref):
    heads = range(STEP_HB)
    slope = slope_ref[...]
    outs, lses = [], []
    for g, (c_ref, (_, dil)) in enumerate(zip((c0_ref, c1_ref, c2_ref), A_GROUPS)):
        q = [q_ref[3 * g, :, i:i + 1] for i in heads]
        s = jnp.concatenate([jnp.sum(c_ref[0, i] * q[i], axis=0, keepdims=True) for i in heads], axis=0)
        s_new = jnp.concatenate([jnp.sum(q_ref[3 * g + 1, :, i:i + 1] * q[i], axis=0, keepdims=True)
                                 for i in heads], axis=0)
        p, p_new, inv, lse = _step_softmax(s, s_new, slope, None, dil)
        outs.append([(jnp.sum(c_ref[1, i] * p[i:i + 1, :], axis=1, keepdims=True)
                      + p_new[i:i + 1, :] * q_ref[3 * g + 2, :, i:i + 1]) * inv[i:i + 1, :] for i in heads])
        lses.append(lse)
    top = jnp.maximum(jnp.maximum(lses[0], lses[1]), lses[2])
    es = [jnp.exp(l - top) for l in lses]
    inv = 1.0 / (es[0] + es[1] + es[2])
    ws = [e * inv for e in es]
    o_ref[...] = jnp.concatenate(
        [sum(ws[g][i:i + 1, :] * outs[g][i] for g in range(len(A_GROUPS))) for i in heads], axis=1)


def _step_a_ffn_kernel(q_ref, c0_ref, c1_ref, c2_ref, slope_ref, x_ref, g_ref, wgu_ref, wd_ref, o_ref, out_ref, *,
                       d_ff):
    _step_a_kernel(q_ref, c0_ref, c1_ref, c2_ref, slope_ref, o_ref)
    _ffn_kernel(x_ref, g_ref, wgu_ref, wd_ref, out_ref, d_ff=d_ff, chunk=d_ff // 2)


def _step_attention_a_with_ffn(qkv, caches, li, x, g, wgu, wd):
    n = qkv.shape[0]
    hb = STEP_HB
    nhb = N_HEADS // hb
    n_seg = 3 * len(A_GROUPS)
    m, d = x.shape
    d_ff = wd.shape[0]
    tm = m // (n * nhb)
    assert tm * n * nhb == m and tm % SUBLANES == 0
    q_t = qkv.reshape(n, n_seg, nhb, hb, HEAD_DIM).transpose(0, 2, 1, 4, 3)
    slope = jnp.asarray(SLOPES, F32).reshape(nhb, hb, 1)
    cache_specs = [pl.BlockSpec((None, None, 2, hb, HEAD_DIM, c.shape[2]), lambda i, j: (li, i, 0, j, 0, 0))
                   for c in caches]
    row_tile = pl.BlockSpec((tm, d), lambda i, j: (i * nhb + j, 0))
    out, x_new = pl.pallas_call(
        functools.partial(_step_a_ffn_kernel, d_ff=d_ff),
        grid=(n, nhb),
        in_specs=[pl.BlockSpec((None, None, n_seg, HEAD_DIM, hb), lambda i, j: (i, j, 0, 0, 0))] + cache_specs
        + [pl.BlockSpec((None, hb, 1), lambda i, j: (j, 0, 0)),
           row_tile, pl.BlockSpec((1, d), lambda i, j: (0, 0)),
           pl.BlockSpec((d, 2 * d_ff), lambda i, j: (0, 0), pipeline_mode=pl.Buffered(1)),
           pl.BlockSpec((d_ff, d), lambda i, j: (0, 0), pipeline_mode=pl.Buffered(1))],
        out_specs=[pl.BlockSpec((None, None, HEAD_DIM, hb), lambda i, j: (i, j, 0, 0)), row_tile],
        out_shape=[jax.ShapeDtypeStruct((n, nhb, HEAD_DIM, hb), F32), jax.ShapeDtypeStruct((m, d), F32)],
        compiler_params=_cparams(("parallel", "parallel")),
        name="step_attn_a_ffn",
    )(q_t, *[_positions_last(c) for c in caches], slope, x, g.reshape(1, d), wgu, wd)
    return out.transpose(0, 1, 3, 2).reshape(n, D_MODEL), x_new


STEP_TB = 8


def _step_b_kernel(q_ref, kn_ref, vn_ref, c_ref, slope_ref, sink_ref, o_ref):
    rep = N_HEADS // HKV_B
    hi = lax.Precision.HIGHEST
    for t in range(q_ref.shape[0]):
        for kv in range(HKV_B):
            hs = slice(kv * rep, (kv + 1) * rep)
            q = q_ref[t, hs, :]
            s = jnp.dot(q, c_ref[t, 0, kv], preferred_element_type=F32, precision=hi)
            s_new = jnp.sum(q * kn_ref[t, kv:kv + 1, :], axis=1, keepdims=True)
            p, p_new, inv, _ = _step_softmax(s, s_new, slope_ref[hs, :], sink_ref[hs, :], 1)
            pv = lax.dot_general(p, c_ref[t, 1, kv], (((1,), (1,)), ((), ())), preferred_element_type=F32,
                                 precision=hi)
            o_ref[t, hs, :] = (pv + p_new * vn_ref[t, kv:kv + 1, :]) * inv


def _step_attention_b(qkv, cache, sink, li):
    n = qkv.shape[0]
    kvw = HKV_B * HEAD_DIM
    tb = min(n, STEP_TB)
    col = lambda v: v.astype(F32).reshape(N_HEADS, 1)
    tok = lambda h: pl.BlockSpec((tb, h, HEAD_DIM), lambda i: (i, 0, 0))
    out = pl.pallas_call(
        _step_b_kernel,
        grid=(n // tb,),
        in_specs=[tok(N_HEADS), tok(HKV_B), tok(HKV_B),
                  pl.BlockSpec((None, tb, 2, HKV_B, HEAD_DIM, cache.shape[2]), lambda i: (li, i, 0, 0, 0, 0)),
                  pl.BlockSpec((N_HEADS, 1), lambda i: (0, 0)), pl.BlockSpec((N_HEADS, 1), lambda i: (0, 0))],
        out_specs=tok(N_HEADS),
        out_shape=jax.ShapeDtypeStruct((n, N_HEADS, HEAD_DIM), F32),
        compiler_params=_cparams(("parallel",)),
        name="step_attn_b",
    )(qkv[:, :D_MODEL].reshape(n, N_HEADS, HEAD_DIM), qkv[:, D_MODEL:D_MODEL + kvw].reshape(n, HKV_B, HEAD_DIM),
      qkv[:, D_MODEL + kvw:].reshape(n, HKV_B, HEAD_DIM), _positions_last(cache), col(jnp.asarray(SLOPES, F32)),
      col(sink))
    return out.reshape(n, D_MODEL)


def _tile_heads(v):
    return jnp.tile(v.astype(F32), N_HEADS)


def _kv_tail_kernel(x_ref, g_ref, wk_ref, wv_ref, gain_ref, seg_ref, o_ref, v_scr):
    h = _rms(x_ref[...], g_ref[...]).astype(BF16)
    k = jnp.dot(h, wk_ref[...], preferred_element_type=F32)
    v_scr[...] = jnp.dot(h, wv_ref[...], preferred_element_type=F32)
    width = k.shape[1]
    sq = (k * k).astype(BF16)
    wc = seg_ref.shape[0]
    parts = [jnp.dot(sq[:, c0:c0 + wc], seg_ref[...], preferred_element_type=F32) for c0 in range(0, width, wc)]
    ssq = parts[0] if len(parts) == 1 else jnp.concatenate(parts, axis=1)
    k = k * lax.rsqrt(ssq * (1.0 / HEAD_DIM) + RMS_EPS) * gain_ref[...]
    o_ref[0, :width, :] = k.T
    o_ref[0, width:, :] = v_scr[...].T


def _kv_tail(x, g, w, gain, *, batch, keep, k_blk, width, heads):
    m, d = x.shape
    s_len = m // batch
    tk = min(keep, 512)
    first = (s_len - keep) // tk
    wseg = min(width, MXU_DIM)
    seg = jnp.asarray(np.kron(np.eye(wseg // HEAD_DIM), np.ones((HEAD_DIM, HEAD_DIM))), BF16)
    out = pl.pallas_call(
        _kv_tail_kernel,
        grid=(batch, keep // tk),
        in_specs=[
            pl.BlockSpec((tk, d), lambda b, j: (b * (s_len // tk) + first + j, 0)),
            pl.BlockSpec((1, d), lambda b, j: (0, 0)),
            pl.BlockSpec((d, width), lambda b, j: (0, k_blk)),
            pl.BlockSpec((d, width), lambda b, j: (0, k_blk + 1)),
            pl.BlockSpec((1, width), lambda b, j: (0, k_blk)),
            pl.BlockSpec((wseg, wseg), lambda b, j: (0, 0)),
        ],
        out_specs=pl.BlockSpec((1, 2 * width, tk), lambda b, j: (b, 0, j)),
        out_shape=jax.ShapeDtypeStruct((batch, 2 * width, keep), F32),
        scratch_shapes=[pltpu.VMEM((tk, width), F32)],
        compiler_params=_cparams(("parallel", "parallel")),
        name="kv_tail",
    )(x, g.reshape(1, d), w, w, gain, seg)
    return out.reshape(batch, 2, heads, HEAD_DIM, keep).transpose(0, 4, 1, 2, 3)


def kernel(x_prompt, x_sample, cache_a_w128, cache_a_w512, cache_a_w2048, cache_b, norm_mix_a, w_in_a, q_gain_a, k_gain_a, w_out_a, norm_ffn_dense, w_gu_dense, w_down_dense, norm_mix_b, w_in_b, q_gain_b, k_gain_b, sink_b, w_out_b, norm_ffn_moe, w_router, w_gu_moe, w_down_moe):
    batch, s_len, d = x_prompt.shape
    n_dec = x_sample.shape[0]
    assert x_sample.shape[1] == 1 and d == D_MODEL
    caches_a = (cache_a_w128, cache_a_w512, cache_a_w2048)
    xp = x_prompt.reshape(batch * s_len, d)
    xs = x_sample.reshape(n_dec, d)
    q_scale = HEAD_DIM ** -0.5
    n_grp = len(A_GROUPS)
    outs = []

    li = 0
    w_in = w_in_a[li].astype(BF16)
    cols_a = w_in.shape[1]
    gain = jnp.concatenate([jnp.concatenate([_tile_heads(q_gain_a[li, g]) * q_scale, _tile_heads(k_gain_a[li, g]),
                                             jnp.ones((d,), F32)]) for g in range(n_grp)]).reshape(1, cols_a)
    flag = jnp.tile(jnp.concatenate([jnp.ones((2 * d,), F32), jnp.zeros((d,), F32)]), n_grp).reshape(1, cols_a)
    qkv_s = _proj(xs, norm_mix_a[li], w_in, gain, flag, batch=1, dil=1, tn=d, col0=0, ncols=cols_a)
    qkv_s = qkv_s.reshape(n_dec, cols_a)
    no_sink = jnp.zeros((N_HEADS,), F32)
    o_p, l_p = [], []
    for g, (window, dil) in enumerate(A_GROUPS):
        qkv_g = _proj(xp, norm_mix_a[li], w_in, gain, flag, batch=batch, dil=dil, tn=d, col0=3 * g, ncols=3 * d)
        o, l = _band_attention(qkv_g, no_sink, q_blk=0, k_blk=1, v_blk=2, kv_heads=N_HEADS, has_sink=False)
        o_p.append(o)
        l_p.append(l)
        outs.append(_kv_tail(xp, norm_mix_a[li], w_in, gain, batch=batch, keep=min(window, s_len), k_blk=3 * g + 1,
                             width=d, heads=N_HEADS)[None])
        kv_s = qkv_s[:, (3 * g + 1) * d:(3 * g + 3) * d]
        outs.append(kv_s.reshape(1, n_dec, 1, 2, N_HEADS, HEAD_DIM).astype(F32))
    w_out = w_out_a[li].astype(BF16)
    xp = _merge_out(o_p, l_p, w_out, xp)
    wgu, wd = w_gu_dense[li].astype(BF16), w_down_dense[li].astype(BF16)
    o_s, xp = _step_attention_a_with_ffn(qkv_s.astype(F32), caches_a, li, xp, norm_ffn_dense[li], wgu, wd)
    xs = _out_proj(o_s, w_out, xs)
    xs = _ffn_dense(xs, norm_ffn_dense[li], wgu, wd)

    w_in = w_in_b[li].astype(BF16)
    cols_b = w_in.shape[1]
    kvw = HKV_B * HEAD_DIM
    gain = jnp.concatenate([_tile_heads(q_gain_b[li]) * q_scale, jnp.tile(k_gain_b[li].astype(F32), HKV_B),
                            jnp.ones((kvw,), F32)]).reshape(1, cols_b)
    flag = jnp.concatenate([jnp.ones((d + kvw,), F32), jnp.zeros((kvw,), F32)]).reshape(1, cols_b)
    qkv_p = _proj(xp, norm_mix_b[li], w_in, gain, flag, batch=batch, dil=1, tn=cols_b, col0=0, ncols=cols_b)
    qkv_s = _proj(xs, norm_mix_b[li], w_in, gain, flag, batch=1, dil=1, tn=cols_b, col0=0, ncols=cols_b)
    qkv_s = qkv_s.reshape(n_dec, cols_b).astype(F32)
    sink = sink_b[li].astype(F32)
    o, _ = _band_attention(qkv_p, sink, q_blk=0, k_blk=d // kvw, v_blk=d // kvw + 1, kv_heads=HKV_B, has_sink=True)
    outs.append(_kv_tail(xp, norm_mix_b[li], w_in, gain, batch=batch, keep=min(N_STEPS, s_len), k_blk=d // kvw,
                         width=kvw, heads=HKV_B)[None])
    outs.append(qkv_s[:, d:].reshape(1, n_dec, 1, 2, HKV_B, HEAD_DIM))
    w_out = w_out_b[li].astype(BF16)
    xp = _out_proj(o.reshape(batch * s_len, d), w_out, xp)
    xs = _out_proj(_step_attention_b(qkv_s, cache_b, sink, li), w_out, xs)
    wgu, wd = w_gu_moe[li].astype(BF16), w_down_moe[li].astype(BF16)
    xp = _moe(xp, norm_ffn_moe[li], w_router[li], wgu, wd, 512)
    xs = _moe(xs, norm_ffn_moe[li], w_router[li], wgu, wd, 128)

    return (xp.reshape(batch, s_len, d), xs.reshape(n_dec, 1, d), *outs)
```

```python
import functools

import numpy as np
import jax
import jax.numpy as jnp
from jax import lax
from jax.experimental import pallas as pl
from jax.experimental.pallas import tpu as pltpu

F32 = jnp.float32
BF16 = jnp.bfloat16

D_MODEL = 1024
HEAD_DIM = 64
N_HEADS = 16
HKV_B = 2
A_GROUPS = ((128, 1), (512, 4), (2048, 16))
N_STEPS = 128
BLK = 128
N_EXPERTS = 8
RMS_EPS = 1e-6
MASKED = -1e30
LANES = 128
SUBLANES = 8
MXU_DIM = 256
VMEM_LIMIT = 56 * 1024 * 1024
D_TILES = D_MODEL // LANES

SLOPES = tuple(float(2.0 ** (-8.0 * (h + 1) / N_HEADS)) for h in range(N_HEADS))


def _cparams(sem):
    return pltpu.CompilerParams(dimension_semantics=sem, vmem_limit_bytes=VMEM_LIMIT)


def _rms(x, g):
    ms = jnp.mean(x * x, axis=-1, keepdims=True)
    return x * lax.rsqrt(ms + RMS_EPS) * g


def _proj_kernel(x_ref, g_ref, w_ref, gain_ref, flag_ref, seg_ref, o_ref, h_scr, *perm_scr, dil):
    @pl.when(pl.program_id(1) == 0)
    def _():
        h = _rms(x_ref[...], g_ref[...])
        if dil == 1:
            h_scr[...] = h.astype(BF16)
        else:
            scr = perm_scr[0]
            n = h.shape[0] // dil
            for c in range(h.shape[1] // LANES):
                cols = slice(c * LANES, (c + 1) * LANES)
                scr[c] = h[:, cols]
                for r in range(dil):
                    h_scr[r * n:(r + 1) * n, cols] = scr[c, pl.ds(r, n, stride=dil), :].astype(BF16)

    acc = jnp.dot(h_scr[...], w_ref[...], preferred_element_type=F32)
    sq = (acc * acc).astype(BF16)
    tm, tn = acc.shape
    ssq = jnp.concatenate(
        [jnp.dot(sq[:, c * MXU_DIM:(c + 1) * MXU_DIM], seg_ref[...], preferred_element_type=F32)
         for c in range(tn // MXU_DIM)], axis=1)
    nrm = acc * lax.rsqrt(ssq * (1.0 / HEAD_DIM) + RMS_EPS) * gain_ref[...]
    res = jnp.where(flag_ref[...] > 0.0, nrm, acc)
    for r in range(dil):
        o_ref[0, r] = res[r * (tm // dil):(r + 1) * (tm // dil)].astype(o_ref.dtype)


def _proj(x, g, w, gain, flag, *, batch, dil, tn, col0, ncols):
    m, d = x.shape
    s_len = m // batch
    tm = min(s_len, 1024)
    tpb = s_len // tm
    seg = jnp.asarray(np.kron(np.eye(MXU_DIM // HEAD_DIM), np.ones((HEAD_DIM, HEAD_DIM))), BF16)
    scratch = [pltpu.VMEM((tm, d), BF16)] + ([pltpu.VMEM((d // LANES, tm, LANES), F32)] if dil > 1 else [])
    return pl.pallas_call(
        functools.partial(_proj_kernel, dil=dil),
        grid=(m // tm, ncols // tn),
        in_specs=[
            pl.BlockSpec((tm, d), lambda i, j: (i, 0)),
            pl.BlockSpec((1, d), lambda i, j: (0, 0)),
            pl.BlockSpec((d, tn), lambda i, j: (0, col0 + j)),
            pl.BlockSpec((1, tn), lambda i, j: (0, col0 + j)),
            pl.BlockSpec((1, tn), lambda i, j: (0, col0 + j)),
            pl.BlockSpec((MXU_DIM, MXU_DIM), lambda i, j: (0, 0)),
        ],
        out_specs=pl.BlockSpec((1, dil, tm // dil, tn), lambda i, j: (i // tpb, 0, i % tpb, j)),
        out_shape=jax.ShapeDtypeStruct((batch, dil, s_len // dil, ncols), BF16),
        scratch_shapes=scratch,
        compiler_params=_cparams(("parallel", "arbitrary")),
        name="proj_qknorm_d%d" % dil,
    )(x, g.reshape(1, d), w, gain, flag, seg)


def _band_kernel(sink_ref, q_ref, kp_ref, kc_ref, vp_ref, vc_ref, o_ref, lse_ref, k_all, v_all, *, dil, qb,
                 kv_heads, has_sink):
    c = pl.program_id(2)
    k_all[0:BLK] = kp_ref[0, 0]
    k_all[BLK:] = kc_ref[0, 0]
    v_all[0:BLK] = vp_ref[0, 0]
    v_all[BLK:] = vc_ref[0, 0]

    qi = lax.broadcasted_iota(jnp.int32, (BLK, 2 * BLK), 0)
    sj = lax.broadcasted_iota(jnp.int32, (BLK, 2 * BLK), 1)
    dist = qi - sj + BLK
    base = jnp.where((dist >= 0) & (dist <= N_STEPS), -(dist.astype(F32) * float(dil)), MASKED)
    base_first = jnp.where(sj >= BLK, base, MASKED)
    low_k = lax.broadcasted_iota(jnp.int32, (2 * BLK, LANES), 1) < HEAD_DIM
    lane_q = lax.broadcasted_iota(jnp.int32, (BLK, LANES), 1)
    low_q = lane_q < HEAD_DIM
    zeros_k = jnp.zeros((2 * BLK, LANES), BF16)
    ones_a = jnp.where(low_k, 1.0, 0.0).astype(BF16)
    ones_b = jnp.where(low_k, 0.0, 1.0).astype(BF16)
    nt = (((1,), (1,)), ((), ()))

    for i in range(qb):
        rows = slice(i * BLK, (i + 1) * BLK)
        win = slice(i * BLK, (i + 2) * BLK)
        base_i = jnp.where(c == 0, base_first, base) if i == 0 else base
        lse_acc = jnp.zeros((BLK, LANES), F32)
        if kv_heads != N_HEADS:
            kw, vw = k_all[win, :], v_all[win, :]
            zero_half = jnp.zeros((2 * BLK, HEAD_DIM), BF16)
            shared = {}
            for kv in range(kv_heads):
                k_kv = kw[:, kv * HEAD_DIM:(kv + 1) * HEAD_DIM]
                v_kv = vw[:, kv * HEAD_DIM:(kv + 1) * HEAD_DIM]
                shared[kv] = tuple((jnp.concatenate([t, zero_half], axis=1), jnp.concatenate([zero_half, t], axis=1))
                                   for t in (k_kv, v_kv))
        for pair in range(N_HEADS // 2):
            pc = slice(pair * LANES, (pair + 1) * LANES)
            ha, hb = 2 * pair, 2 * pair + 1
            qp = q_ref[0, 0, rows, pc]
            if kv_heads == N_HEADS:
                kw, vw = k_all[win, pc], v_all[win, pc]
                k_a, k_b = jnp.where(low_k, kw, zeros_k), jnp.where(low_k, zeros_k, kw)
                v_a, v_b = jnp.where(low_k, vw, zeros_k), jnp.where(low_k, zeros_k, vw)
            else:
                (k_a, k_b), (v_a, v_b) = shared[ha // (N_HEADS // kv_heads)]
            s_a = lax.dot_general(qp, k_a, nt, preferred_element_type=F32) + SLOPES[ha] * base_i
            s_b = lax.dot_general(qp, k_b, nt, preferred_element_type=F32) + SLOPES[hb] * base_i
            m_a = jnp.max(s_a, axis=-1, keepdims=True)
            m_b = jnp.max(s_b, axis=-1, keepdims=True)
            if has_sink:
                m_a = jnp.maximum(m_a, sink_ref[ha])
                m_b = jnp.maximum(m_b, sink_ref[hb])
            p_a = jnp.exp(s_a - m_a).astype(BF16)
            p_b = jnp.exp(s_b - m_b).astype(BF16)
            res = (jnp.dot(p_a, jnp.concatenate([v_a, ones_a], axis=1), preferred_element_type=F32)
                   + jnp.dot(p_b, jnp.concatenate([v_b, ones_b], axis=1), preferred_element_type=F32))
            den = res[:, LANES:]
            m_pair = jnp.where(low_q, m_a, m_b)
            if has_sink:
                den = den + jnp.where(low_q, jnp.exp(sink_ref[ha] - m_a), jnp.exp(sink_ref[hb] - m_b))
            o_ref[0, 0, rows, pc] = (res[:, :LANES] * (1.0 / den)).astype(o_ref.dtype)
            lse_acc = jnp.where((lane_q & (HEAD_DIM - 1)) == pair, m_pair + jnp.log(den), lse_acc)
        lse_ref[0, 0, rows, :] = lse_acc


def _band_attention(qkv, sink, *, q_blk, k_blk, v_blk, kv_heads, has_sink):
    batch, dil, ls, _ = qkv.shape
    qb = min(4, ls // BLK)
    rows = BLK * qb
    kvw = kv_heads * HEAD_DIM
    kern = functools.partial(_band_kernel, dil=dil, qb=qb, kv_heads=kv_heads, has_sink=has_sink)
    prev = lambda b, r, c: jnp.maximum(c * qb - 1, 0)
    return pl.pallas_call(
        kern,
        grid=(batch, dil, ls // rows),
        in_specs=[
            pl.BlockSpec(memory_space=pltpu.SMEM),
            pl.BlockSpec((1, 1, rows, D_MODEL), lambda b, r, c: (b, r, c, q_blk)),
            pl.BlockSpec((1, 1, BLK, kvw), lambda b, r, c: (b, r, prev(b, r, c), k_blk)),
            pl.BlockSpec((1, 1, rows, kvw), lambda b, r, c: (b, r, c, k_blk)),
            pl.BlockSpec((1, 1, BLK, kvw), lambda b, r, c: (b, r, prev(b, r, c), v_blk)),
            pl.BlockSpec((1, 1, rows, kvw), lambda b, r, c: (b, r, c, v_blk)),
        ],
        out_specs=[pl.BlockSpec((1, 1, rows, D_MODEL), lambda b, r, c: (b, r, c, 0)),
                   pl.BlockSpec((1, 1, rows, LANES), lambda b, r, c: (b, r, c, 0))],
        out_shape=[jax.ShapeDtypeStruct((batch, dil, ls, D_MODEL), BF16),
                   jax.ShapeDtypeStruct((batch, dil, ls, LANES), F32)],
        scratch_shapes=[pltpu.VMEM((BLK + rows, kvw), BF16), pltpu.VMEM((BLK + rows, kvw), BF16)],
        compiler_params=_cparams(("parallel", "parallel", "arbitrary")),
        name="band_attn_d%d" % dil,
    )(sink, qkv, qkv, qkv, qkv, qkv)


def _lse_lane(h):
    return h // 2 + HEAD_DIM * (h % 2)


def _head_expand():
    e = np.zeros((LANES, D_MODEL), np.float32)
    for h in range(N_HEADS):
        e[_lse_lane(h), h * HEAD_DIM:(h + 1) * HEAD_DIM] = 1.0
    return jnp.asarray(e, BF16)


def _split_dot(a, b):
    hi = a.astype(BF16)
    lo = (a - hi.astype(F32)).astype(BF16)
    return jnp.dot(hi, b, preferred_element_type=F32) + jnp.dot(lo, b, preferred_element_type=F32)


def _merge_out_kernel(*refs, dils):
    o_refs, l_refs = refs[0:3], refs[3:6]
    e_ref, w_ref, x_ref, out_ref = refs[6:10]
    scr = refs[10:]
    tm = x_ref.shape[0]
    outs, lses = [], []
    k = 0
    for o_ref, l_ref, dil in zip(o_refs, l_refs, dils):
        if dil == 1:
            outs.append(o_ref[0, 0].astype(F32))
            lses.append(l_ref[0, 0])
        else:
            so, sl = scr[k], scr[k + 1]
            k += 2
            for r in range(dil):
                rows = pl.ds(r, tm // dil, stride=dil)
                sl[rows, :] = l_ref[0, r]
                for c in range(D_TILES):
                    so[c, rows, :] = o_ref[0, r, :, c * LANES:(c + 1) * LANES].astype(F32)
            outs.append(jnp.concatenate([so[c] for c in range(D_TILES)], axis=1))
            lses.append(sl[...])
    top = jnp.maximum(jnp.maximum(lses[0], lses[1]), lses[2])
    es = [jnp.exp(l - top) for l in lses]
    inv = 1.0 / (es[0] + es[1] + es[2])
    mix = sum(_split_dot(e * inv, e_ref[...]) * o for e, o in zip(es, outs))
    out_ref[...] = x_ref[...] + jnp.dot(mix.astype(BF16), w_ref[...], preferred_element_type=F32)


def _merge_out(outs, lses, w, x):
    m, d = x.shape
    batch = outs[0].shape[0]
    s_len = m // batch
    tm = min(s_len, 256)
    tpb = s_len // tm
    dils = tuple(o.shape[1] for o in outs)
    grp = lambda width: [pl.BlockSpec((1, dil, tm // dil, width), lambda i: (i // tpb, 0, i % tpb, 0))
                         for dil in dils]
    scratch = [s for dil in dils if dil > 1
               for s in (pltpu.VMEM((D_TILES, tm, LANES), F32), pltpu.VMEM((tm, LANES), F32))]
    return pl.pallas_call(
        functools.partial(_merge_out_kernel, dils=dils),
        grid=(m // tm,),
        in_specs=grp(d) + grp(LANES) + [pl.BlockSpec((LANES, d), lambda i: (0, 0)),
                                         pl.BlockSpec((d, d), lambda i: (0, 0)),
                                         pl.BlockSpec((tm, d), lambda i: (i, 0))],
        out_specs=pl.BlockSpec((tm, d), lambda i: (i, 0)),
        out_shape=jax.ShapeDtypeStruct((m, d), F32),
        scratch_shapes=scratch,
        compiler_params=_cparams(("parallel",)),
        name="merge_outproj",
    )(*outs, *lses, _head_expand(), w, x)


def _out_kernel(o_ref, w_ref, x_ref, out_ref):
    out_ref[...] = x_ref[...] + jnp.dot(o_ref[...].astype(BF16), w_ref[...], preferred_element_type=F32)


def _out_proj(o, w, x):
    m, d = x.shape
    tm = min(m, 1024)
    return pl.pallas_call(
        _out_kernel,
        grid=(m // tm,),
        in_specs=[pl.BlockSpec((tm, d), lambda i: (i, 0)), pl.BlockSpec((d, d), lambda i: (0, 0)),
                  pl.BlockSpec((tm, d), lambda i: (i, 0))],
        out_specs=pl.BlockSpec((tm, d), lambda i: (i, 0)),
        out_shape=jax.ShapeDtypeStruct((m, d), F32),
        compiler_params=_cparams(("parallel",)),
        name="outproj",
    )(o, w, x)


def _silu(g):
    return g * (1.0 / (1.0 + jnp.exp(-g)))


def _ffn_kernel(x_ref, g_ref, wgu_ref, wd_ref, out_ref, *, d_ff, chunk):
    x = x_ref[...]
    h = _rms(x, g_ref[...]).astype(BF16)
    acc = jnp.zeros(x.shape, F32)
    for c in range(d_ff // chunk):
        gate = jnp.dot(h, wgu_ref[:, c * chunk:(c + 1) * chunk], preferred_element_type=F32)
        up = jnp.dot(h, wgu_ref[:, d_ff + c * chunk:d_ff + (c + 1) * chunk], preferred_element_type=F32)
        act = (_silu(gate) * up).astype(BF16)
        acc = acc + jnp.dot(act, wd_ref[c * chunk:(c + 1) * chunk, :], preferred_element_type=F32)
    out_ref[...] = x + acc


def _ffn_dense(x, g, wgu, wd):
    m, d = x.shape
    tm = min(m, 512)
    d_ff = wd.shape[0]
    kern = functools.partial(_ffn_kernel, d_ff=d_ff, chunk=d_ff // 2)
    return pl.pallas_call(
        kern,
        grid=(m // tm,),
        in_specs=[pl.BlockSpec((tm, d), lambda i: (i, 0)), pl.BlockSpec((1, d), lambda i: (0, 0)),
                  pl.BlockSpec((d, 2 * d_ff), lambda i: (0, 0), pipeline_mode=pl.Buffered(1)),
                  pl.BlockSpec((d_ff, d), lambda i: (0, 0), pipeline_mode=pl.Buffered(1))],
        out_specs=pl.BlockSpec((tm, d), lambda i: (i, 0)),
        out_shape=jax.ShapeDtypeStruct((m, d), F32),
        compiler_params=_cparams(("parallel",)),
        name="ffn_dense",
    )(x, g.reshape(1, d), wgu, wd)


def _router_kernel(x_ref, g_ref, wr_ref, idx_ref, gate_ref):
    h = _rms(x_ref[...], g_ref[...])
    logits = jnp.dot(h, wr_ref[...], preferred_element_type=F32, precision=lax.Precision.HIGHEST)
    lane = lax.broadcasted_iota(jnp.int32, logits.shape, 1)
    logits = jnp.where(lane < N_EXPERTS, logits, -jnp.inf)
    t1 = jnp.max(logits, axis=-1, keepdims=True)
    i1 = jnp.min(jnp.where(logits == t1, lane, LANES), axis=-1, keepdims=True)
    rest = jnp.where(lane == i1, -jnp.inf, logits)
    t2 = jnp.max(rest, axis=-1, keepdims=True)
    i2 = jnp.min(jnp.where(rest == t2, lane, LANES), axis=-1, keepdims=True)
    e2 = jnp.exp(t2 - t1)
    g1 = 1.0 / (1.0 + e2)
    g2 = e2 / (1.0 + e2)
    idx_ref[...] = jnp.where(lane == 0, i1, jnp.where(lane == 1, i2, 0))
    gate_ref[...] = jnp.where(lane == 0, g1, jnp.where(lane == 1, g2, 0.0))


def _router(x, g, w_router):
    m, d = x.shape
    tm = min(m, 1024)
    wr = jnp.zeros((d, LANES), F32).at[:, :N_EXPERTS].set(w_router)
    return pl.pallas_call(
        _router_kernel,
        grid=(m // tm,),
        in_specs=[pl.BlockSpec((tm, d), lambda i: (i, 0)), pl.BlockSpec((1, d), lambda i: (0, 0)),
                  pl.BlockSpec((d, LANES), lambda i: (0, 0))],
        out_specs=[pl.BlockSpec((tm, LANES), lambda i: (i, 0))] * 2,
        out_shape=[jax.ShapeDtypeStruct((m, LANES), jnp.int32), jax.ShapeDtypeStruct((m, LANES), F32)],
        compiler_params=_cparams(("parallel",)),
        name="router_top2",
    )(x, g.reshape(1, d), wr)


ROW_UNROLL = 4


def _row_copy(src, dst, sem, src_row, dst_row):
    return pltpu.make_async_copy(src.at[pl.ds(src_row, 1)], dst.at[pl.ds(dst_row, 1)], sem)


ZERO_ROWS = 128


def _dispatch_kernel(pad_ref, pos_ref, x_ref, xs_hbm, zeros, sem, *, rows):
    @pl.when(pl.program_id(0) == 0)
    def _():
        zeros[...] = jnp.zeros(zeros.shape, zeros.dtype)

        def pad_start(r, carry):
            _row_copy(zeros, xs_hbm, sem, 0, r).start()
            return carry

        def pad_wait(r, carry):
            _row_copy(zeros, xs_hbm, sem, 0, r).wait()
            return carry

        for e in range(N_EXPERTS):
            lax.fori_loop(pad_ref[e], pad_ref[N_EXPERTS + e], pad_start, 0)
            lax.fori_loop(pad_ref[e], pad_ref[N_EXPERTS + e], pad_wait, 0)

        def tail_copy(b):
            return pltpu.make_async_copy(zeros, xs_hbm.at[pl.ds(b * ZERO_ROWS, ZERO_ROWS)], sem)

        def tail_start(b, carry):
            tail_copy(b).start()
            return carry

        def tail_wait(b, carry):
            tail_copy(b).wait()
            return carry

        first, last = pad_ref[2 * N_EXPERTS - 1] // ZERO_ROWS, xs_hbm.shape[0] // ZERO_ROWS
        lax.fori_loop(first, last, tail_start, 0)
        lax.fori_loop(first, last, tail_wait, 0)

    def start(t, carry):
        for u in range(ROW_UNROLL):
            r = t * ROW_UNROLL + u
            _row_copy(x_ref, xs_hbm, sem, r, pos_ref[0, 0, 2 * r]).start(priority=0)
            _row_copy(x_ref, xs_hbm, sem, r, pos_ref[0, 0, 2 * r + 1]).start(priority=1)
        return carry

    def wait(t, carry):
        for u in range(2 * ROW_UNROLL):
            _row_copy(x_ref, xs_hbm, sem, 0, 0).wait()
        return carry

    lax.fori_loop(0, rows // ROW_UNROLL, start, 0)
    lax.fori_loop(0, rows // ROW_UNROLL, wait, 0)


def _dispatch(x, pos, pad, n_rows, rows):
    m, d = x.shape
    assert n_rows % ZERO_ROWS == 0
    return pl.pallas_call(
        functools.partial(_dispatch_kernel, rows=rows),
        grid_spec=pltpu.PrefetchScalarGridSpec(
            num_scalar_prefetch=1,
            grid=(m // rows,),
            in_specs=[pl.BlockSpec((1, 1, 2 * rows), lambda i, pad_ref: (i, 0, 0), memory_space=pltpu.SMEM),
                      pl.BlockSpec((rows, d), lambda i, pad_ref: (i, 0))],
            out_specs=pl.BlockSpec(memory_space=pl.ANY),
            scratch_shapes=[pltpu.VMEM((ZERO_ROWS, d), F32), pltpu.SemaphoreType.DMA(())],
        ),
        out_shape=jax.ShapeDtypeStruct((n_rows, d), x.dtype),
        compiler_params=_cparams(("arbitrary",)),
        name="moe_dispatch",
    )(pad, pos.reshape(m // rows, 1, 2 * rows), x)


def _expert_kernel(te_ref, nv_ref, xs_ref, g_ref, wgu_ref, wd_ref, o_ref, *, d_ff):
    @pl.when(pl.program_id(0) < nv_ref[0])
    def _():
        h = _rms(xs_ref[...], g_ref[...]).astype(BF16)
        gate = jnp.dot(h, wgu_ref[0, :, :d_ff], preferred_element_type=F32)
        up = jnp.dot(h, wgu_ref[0, :, d_ff:], preferred_element_type=F32)
        act = (_silu(gate) * up).astype(BF16)
        o_ref[...] = jnp.dot(act, wd_ref[0], preferred_element_type=F32)

    @pl.when(pl.program_id(0) >= nv_ref[0])
    def _():
        o_ref[...] = jnp.zeros(o_ref.shape, o_ref.dtype)


def _experts(xs, g, tile_expert, n_valid, wgu, wd, tm):
    n, d = xs.shape
    d_ff = wd.shape[1]
    row_tile = lambda i, te, nv: (jnp.minimum(i, nv[0] - 1), 0)
    return pl.pallas_call(
        functools.partial(_expert_kernel, d_ff=d_ff),
        grid_spec=pltpu.PrefetchScalarGridSpec(
            num_scalar_prefetch=2,
            grid=(n // tm,),
            in_specs=[
                pl.BlockSpec((tm, d), row_tile),
                pl.BlockSpec((1, d), lambda i, te, nv: (0, 0)),
                pl.BlockSpec((1, d, 2 * d_ff), lambda i, te, nv: (te[i], 0, 0)),
                pl.BlockSpec((1, d_ff, d), lambda i, te, nv: (te[i], 0, 0)),
            ],
            out_specs=pl.BlockSpec((tm, d), lambda i, te, nv: (i, 0)),
        ),
        out_shape=jax.ShapeDtypeStruct((n, d), F32),
        compiler_params=_cparams(("arbitrary",)),
        name="moe_experts",
    )(tile_expert, n_valid, xs, g.reshape(1, d), wgu, wd)


def _combine_kernel(pos_ref, x_ref, gate_ref, y_hbm, o_ref, buf0, buf1, sem, *, rows):
    def start(t, carry):
        for u in range(ROW_UNROLL):
            r = t * ROW_UNROLL + u
            _row_copy(y_hbm, buf0, sem.at[0], pos_ref[0, 0, 2 * r], r).start(priority=0)
            _row_copy(y_hbm, buf1, sem.at[1], pos_ref[0, 0, 2 * r + 1], r).start(priority=1)
        return carry

    def wait(t, carry):
        for u in range(ROW_UNROLL):
            r = t * ROW_UNROLL + u
            _row_copy(y_hbm, buf0, sem.at[0], 0, r).wait()
            _row_copy(y_hbm, buf1, sem.at[1], 0, r).wait()
        return carry

    lax.fori_loop(0, rows // ROW_UNROLL, start, 0)
    lax.fori_loop(0, rows // ROW_UNROLL, wait, 0)
    o_ref[...] = x_ref[...] + (gate_ref[:, 0:1] * buf0[...] + gate_ref[:, 1:2] * buf1[...])


def _combine(x, ys, pos, gate, rows):
    m, d = x.shape
    return pl.pallas_call(
        functools.partial(_combine_kernel, rows=rows),
        grid=(m // rows,),
        in_specs=[pl.BlockSpec((1, 1, 2 * rows), lambda i: (i, 0, 0), memory_space=pltpu.SMEM),
                  pl.BlockSpec((rows, d), lambda i: (i, 0)), pl.BlockSpec((rows, LANES), lambda i: (i, 0)),
                  pl.BlockSpec(memory_space=pl.ANY)],
        out_specs=pl.BlockSpec((rows, d), lambda i: (i, 0)),
        scratch_shapes=[pltpu.VMEM((rows, d), F32), pltpu.VMEM((rows, d), F32), pltpu.SemaphoreType.DMA((2,))],
        out_shape=jax.ShapeDtypeStruct((m, d), F32),
        compiler_params=_cparams(("arbitrary",)),
        name="moe_combine",
    )(pos.reshape(m // rows, 1, 2 * rows), x, gate, ys)


def _moe(x, g, w_router, wgu, wd, tm):
    m = x.shape[0]
    idx, gate = _router(x, g, w_router)
    e_flat = idx[:, :2].reshape(-1)
    onehot = (e_flat[:, None] == jnp.arange(N_EXPERTS)[None, :]).astype(jnp.int32)
    counts = jnp.sum(onehot, axis=0)
    rank = jnp.sum((jnp.cumsum(onehot, axis=0) - onehot) * onehot, axis=1)
    padded = ((counts + tm - 1) // tm) * tm
    ends = jnp.cumsum(padded)
    starts = ends - padded
    pos = (jnp.sum(onehot * starts[None, :], axis=1) + rank).astype(jnp.int32)
    pad = jnp.concatenate([starts + counts, ends]).astype(jnp.int32)
    n_tiles = -(-(2 * m + N_EXPERTS * (tm - 1)) // tm)
    tile_start = jnp.arange(n_tiles, dtype=jnp.int32) * tm
    tile_expert = jnp.minimum(jnp.sum((tile_start[:, None] >= ends[None, :]).astype(jnp.int32), axis=1),
                              N_EXPERTS - 1).astype(jnp.int32)
    n_valid = (ends[-1] // tm).astype(jnp.int32).reshape(1)

    rows = min(m, 512)
    xs = _dispatch(x, pos, pad, n_tiles * tm, rows)
    ys = _experts(xs, g, tile_expert, n_valid, wgu, wd, tm)
    return _combine(x, ys, pos, gate, rows)


STEP_HB = 8


def _positions_last(cache):
    return cache.transpose(0, 1, 3, 4, 5, 2)


def _step_softmax(s, s_new, slope, sink, dil):
    length = s.shape[1]
    pos = lax.broadcasted_iota(jnp.int32, (1, length), 1)
    s = s - slope * (length - pos).astype(F32)
    if dil > 1:
        s = jnp.where((pos & (dil - 1)) == 0, s, MASKED)
    m = jnp.maximum(jnp.max(s, axis=1, keepdims=True), s_new)
    if sink is not None:
        m = jnp.maximum(m, sink)
    p = jnp.exp(s - m)
    p_new = jnp.exp(s_new - m)
    den = jnp.sum(p, axis=1, keepdims=True) + p_new
    if sink is not None:
        den = den + jnp.exp(sink - m)
    return p, p_new, 1.0 / den, m + jnp.log(den)


def _step_a_kernel(q_ref, c0_ref, c1_ref, c2_ref, slope_ref, o_ref):
    heads = range(STEP_HB)
    slope = slope_ref[...]
    eye = (lax.broadcasted_iota(jnp.int32, (HEAD_DIM, HEAD_DIM), 0)
           == lax.broadcasted_iota(jnp.int32, (HEAD_DIM, HEAD_DIM), 1)).astype(F32)
    cols = lambda seg: lax.dot_general(eye, q_ref[seg], (((1,), (1,)), ((), ())), preferred_element_type=F32,
                                       precision=lax.Precision.HIGHEST)
    outs, lses = [], []
    for g, (c_ref, (_, dil)) in enumerate(zip((c0_ref, c1_ref, c2_ref), A_GROUPS)):
        q_t, kn_t, vn_t = cols(3 * g), cols(3 * g + 1), cols(3 * g + 2)
        q = [q_t[:, i:i + 1] for i in heads]
        s = jnp.concatenate([jnp.sum(c_ref[0, i] * q[i], axis=0, keepdims=True) for i in heads], axis=0)
        s_new = jnp.concatenate([jnp.sum(kn_t[:, i:i + 1] * q[i], axis=0, keepdims=True) for i in heads], axis=0)
        p, p_new, inv, lse = _step_softmax(s, s_new, slope, None, dil)
        outs.append([(jnp.sum(c_ref[1, i] * p[i:i + 1, :], axis=1, keepdims=True)
                      + p_new[i:i + 1, :] * vn_t[:, i:i + 1]) * inv[i:i + 1, :] for i in heads])
        lses.append(lse)
    top = jnp.maximum(jnp.maximum(lses[0], lses[1]), lses[2])
    es = [jnp.exp(l - top) for l in lses]
    inv = 1.0 / (es[0] + es[1] + es[2])
    ws = [e * inv for e in es]
    o_ref[...] = jnp.concatenate(
        [sum(ws[g][i:i + 1, :] * outs[g][i] for g in range(len(A_GROUPS))) for i in heads], axis=1)


def _step_a_ffn_kernel(q_ref, c0_ref, c1_ref, c2_ref, slope_ref, x_ref, g_ref, wgu_ref, wd_ref, o_ref, out_ref, *,
                       d_ff):
    _step_a_kernel(q_ref, c0_ref, c1_ref, c2_ref, slope_ref, o_ref)
    _ffn_kernel(x_ref, g_ref, wgu_ref, wd_ref, out_ref, d_ff=d_ff, chunk=d_ff // 2)


def _step_attention_a_with_ffn(qkv, caches, li, x, g, wgu, wd):
    n = qkv.shape[0]
    hb = STEP_HB
    nhb = N_HEADS // hb
    n_seg = 3 * len(A_GROUPS)
    m, d = x.shape
    d_ff = wd.shape[0]
    tm = m // (n * nhb)
    assert tm * n * nhb == m and tm % SUBLANES == 0
    q_t = qkv.reshape(n, n_seg, nhb, hb, HEAD_DIM)
    slope = jnp.asarray(SLOPES, F32).reshape(nhb, hb, 1)
    cache_specs = [pl.BlockSpec((None, None, 2, hb, HEAD_DIM, c.shape[2]), lambda i, j: (li, i, 0, j, 0, 0))
                   for c in caches]
    row_tile = pl.BlockSpec((tm, d), lambda i, j: (i * nhb + j, 0))
    out, x_new = pl.pallas_call(
        functools.partial(_step_a_ffn_kernel, d_ff=d_ff),
        grid=(n, nhb),
        in_specs=[pl.BlockSpec((None, n_seg, None, hb, HEAD_DIM), lambda i, j: (i, 0, j, 0, 0))] + cache_specs
        + [pl.BlockSpec((None, hb, 1), lambda i, j: (j, 0, 0)),
           row_tile, pl.BlockSpec((1, d), lambda i, j: (0, 0)),
           pl.BlockSpec((d, 2 * d_ff), lambda i, j: (0, 0), pipeline_mode=pl.Buffered(1)),
           pl.BlockSpec((d_ff, d), lambda i, j: (0, 0), pipeline_mode=pl.Buffered(1))],
        out_specs=[pl.BlockSpec((None, None, HEAD_DIM, hb), lambda i, j: (i, j, 0, 0)), row_tile],
        out_shape=[jax.ShapeDtypeStruct((n, nhb, HEAD_DIM, hb), F32), jax.ShapeDtypeStruct((m, d), F32)],
        compiler_params=_cparams(("parallel", "parallel")),
        name="step_attn_a_ffn",
    )(q_t, *[_positions_last(c) for c in caches], slope, x, g.reshape(1, d), wgu, wd)
    return out.transpose(0, 1, 3, 2).reshape(n, D_MODEL), x_new


STEP_TB = 8


def _step_b_kernel(q_ref, kn_ref, vn_ref, c_ref, slope_ref, sink_ref, o_ref):
    rep = N_HEADS // HKV_B
    hi = lax.Precision.HIGHEST
    for t in range(q_ref.shape[0]):
        for kv in range(HKV_B):
            hs = slice(kv * rep, (kv + 1) * rep)
            q = q_ref[t, hs, :]
            s = jnp.dot(q, c_ref[t, 0, kv], preferred_element_type=F32, precision=hi)
            s_new = jnp.sum(q * kn_ref[t, kv:kv + 1, :], axis=1, keepdims=True)
            p, p_new, inv, _ = _step_softmax(s, s_new, slope_ref[hs, :], sink_ref[hs, :], 1)
            pv = lax.dot_general(p, c_ref[t, 1, kv], (((1,), (1,)), ((), ())), preferred_element_type=F32,
                                 precision=hi)
            o_ref[t, hs, :] = (pv + p_new * vn_ref[t, kv:kv + 1, :]) * inv


def _step_attention_b(qkv, cache, sink, li):
    n = qkv.shape[0]
    kvw = HKV_B * HEAD_DIM
    tb = min(n, STEP_TB)
    col = lambda v: v.astype(F32).reshape(N_HEADS, 1)
    tok = lambda h: pl.BlockSpec((tb, h, HEAD_DIM), lambda i: (i, 0, 0))
    out = pl.pallas_call(
        _step_b_kernel,
        grid=(n // tb,),
        in_specs=[tok(N_HEADS), tok(HKV_B), tok(HKV_B),
                  pl.BlockSpec((None, tb, 2, HKV_B, HEAD_DIM, cache.shape[2]), lambda i: (li, i, 0, 0, 0, 0)),
                  pl.BlockSpec((N_HEADS, 1), lambda i: (0, 0)), pl.BlockSpec((N_HEADS, 1), lambda i: (0, 0))],
        out_specs=tok(N_HEADS),
        out_shape=jax.ShapeDtypeStruct((n, N_HEADS, HEAD_DIM), F32),
        compiler_params=_cparams(("parallel",)),
        name="step_attn_b",
    )(qkv[:, :D_MODEL].reshape(n, N_HEADS, HEAD_DIM), qkv[:, D_MODEL:D_MODEL + kvw].reshape(n, HKV_B, HEAD_DIM),
      qkv[:, D_MODEL + kvw:].reshape(n, HKV_B, HEAD_DIM), _positions_last(cache), col(jnp.asarray(SLOPES, F32)),
      col(sink))
    return out.reshape(n, D_MODEL)


def _tile_heads(v):
    return jnp.tile(v.astype(F32), N_HEADS)


def _kv_tail_kernel(x_ref, g_ref, wk_ref, wv_ref, gain_ref, seg_ref, o_ref, v_scr):
    h = _rms(x_ref[...], g_ref[...]).astype(BF16)
    k = jnp.dot(h, wk_ref[...], preferred_element_type=F32)
    v_scr[...] = jnp.dot(h, wv_ref[...], preferred_element_type=F32)
    width = k.shape[1]
    sq = (k * k).astype(BF16)
    wc = seg_ref.shape[0]
    parts = [jnp.dot(sq[:, c0:c0 + wc], seg_ref[...], preferred_element_type=F32) for c0 in range(0, width, wc)]
    ssq = parts[0] if len(parts) == 1 else jnp.concatenate(parts, axis=1)
    k = k * lax.rsqrt(ssq * (1.0 / HEAD_DIM) + RMS_EPS) * gain_ref[...]
    o_ref[0, :width, :] = k.T
    o_ref[0, width:, :] = v_scr[...].T


def _kv_tail(x, g, w, gain, *, batch, keep, k_blk, width, heads):
    m, d = x.shape
    s_len = m // batch
    tk = min(keep, 512)
    first = (s_len - keep) // tk
    wseg = min(width, MXU_DIM)
    seg = jnp.asarray(np.kron(np.eye(wseg // HEAD_DIM), np.ones((HEAD_DIM, HEAD_DIM))), BF16)
    out = pl.pallas_call(
        _kv_tail_kernel,
        grid=(batch, keep // tk),
        in_specs=[
            pl.BlockSpec((tk, d), lambda b, j: (b * (s_len // tk) + first + j, 0)),
            pl.BlockSpec((1, d), lambda b, j: (0, 0)),
            pl.BlockSpec((d, width), lambda b, j: (0, k_blk)),
            pl.BlockSpec((d, width), lambda b, j: (0, k_blk + 1)),
            pl.BlockSpec((1, width), lambda b, j: (0, k_blk)),
            pl.BlockSpec((wseg, wseg), lambda b, j: (0, 0)),
        ],
        out_specs=pl.BlockSpec((1, 2 * width, tk), lambda b, j: (b, 0, j)),
        out_shape=jax.ShapeDtypeStruct((batch, 2 * width, keep), F32),
        scratch_shapes=[pltpu.VMEM((tk, width), F32)],
        compiler_params=_cparams(("parallel", "parallel")),
        name="kv_tail",
    )(x, g.reshape(1, d), w, w, gain, seg)
    return out.reshape(batch, 2, heads, HEAD_DIM, keep).transpose(0, 4, 1, 2, 3)


def kernel(x_prompt, x_sample, cache_a_w128, cache_a_w512, cache_a_w2048, cache_b, norm_mix_a, w_in_a, q_gain_a, k_gain_a, w_out_a, norm_ffn_dense, w_gu_dense, w_down_dense, norm_mix_b, w_in_b, q_gain_b, k_gain_b, sink_b, w_out_b, norm_ffn_moe, w_router, w_gu_moe, w_down_moe):
    batch, s_len, d = x_prompt.shape
    n_dec = x_sample.shape[0]
    assert x_sample.shape[1] == 1 and d == D_MODEL
    caches_a = (cache_a_w128, cache_a_w512, cache_a_w2048)
    xp = x_prompt.reshape(batch * s_len, d)
    xs = x_sample.reshape(n_dec, d)
    q_scale = HEAD_DIM ** -0.5
    n_grp = len(A_GROUPS)
    outs = []

    li = 0
    w_in = w_in_a[li].astype(BF16)
    cols_a = w_in.shape[1]
    gain = jnp.concatenate([jnp.concatenate([_tile_heads(q_gain_a[li, g]) * q_scale, _tile_heads(k_gain_a[li, g]),
                                             jnp.ones((d,), F32)]) for g in range(n_grp)]).reshape(1, cols_a)
    flag = jnp.tile(jnp.concatenate([jnp.ones((2 * d,), F32), jnp.zeros((d,), F32)]), n_grp).reshape(1, cols_a)
    qkv_s = _proj(xs, norm_mix_a[li], w_in, gain, flag, batch=1, dil=1, tn=d, col0=0, ncols=cols_a)
    qkv_s = qkv_s.reshape(n_dec, cols_a)
    no_sink = jnp.zeros((N_HEADS,), F32)
    o_p, l_p = [], []
    for g, (window, dil) in enumerate(A_GROUPS):
        qkv_g = _proj(xp, norm_mix_a[li], w_in, gain, flag, batch=batch, dil=dil, tn=d, col0=3 * g, ncols=3 * d)
        o, l = _band_attention(qkv_g, no_sink, q_blk=0, k_blk=1, v_blk=2, kv_heads=N_HEADS, has_sink=False)
        o_p.append(o)
        l_p.append(l)
        outs.append(_kv_tail(xp, norm_mix_a[li], w_in, gain, batch=batch, keep=min(window, s_len), k_blk=3 * g + 1,
                             width=d, heads=N_HEADS)[None])
        kv_s = qkv_s[:, (3 * g + 1) * d:(3 * g + 3) * d]
        outs.append(kv_s.reshape(1, n_dec, 1, 2, N_HEADS, HEAD_DIM).astype(F32))
    w_out = w_out_a[li].astype(BF16)
    xp = _merge_out(o_p, l_p, w_out, xp)
    wgu, wd = w_gu_dense[li].astype(BF16), w_down_dense[li].astype(BF16)
    o_s, xp = _step_attention_a_with_ffn(qkv_s.astype(F32), caches_a, li, xp, norm_ffn_dense[li], wgu, wd)
    xs = _out_proj(o_s, w_out, xs)
    xs = _ffn_dense(xs, norm_ffn_dense[li], wgu, wd)

    w_in = w_in_b[li].astype(BF16)
    cols_b = w_in.shape[1]
    kvw = HKV_B * HEAD_DIM
    gain = jnp.concatenate([_tile_heads(q_gain_b[li]) * q_scale, jnp.tile(k_gain_b[li].astype(F32), HKV_B),
                            jnp.ones((kvw,), F32)]).reshape(1, cols_b)
    flag = jnp.concatenate([jnp.ones((d + kvw,), F32), jnp.zeros((kvw,), F32)]).reshape(1, cols_b)
    qkv_p = _proj(xp, norm_mix_b[li], w_in, gain, flag, batch=batch, dil=1, tn=cols_b, col0=0, ncols=cols_b)
    qkv_s = _proj(xs, norm_mix_b[li], w_in, gain, flag, batch=1, dil=1, tn=cols_b, col0=0, ncols=cols_b)
    qkv_s = qkv_s.reshape(n_dec, cols_b).astype(F32)
    sink = sink_b[li].astype(F32)
    o, _ = _band_attention(qkv_p, sink, q_blk=0, k_blk=d // kvw, v_blk=d // kvw + 1, kv_heads=HKV_B, has_sink=True)
    outs.append(_kv_tail(xp, norm_mix_b[li], w_in, gain, batch=batch, keep=min(N_STEPS, s_len), k_blk=d // kvw,
                         width=kvw, heads=HKV_B)[None])
    outs.append(qkv_s[:, d:].reshape(1, n_dec, 1, 2, HKV_B, HEAD_DIM))
    w_out = w_out_b[li].astype(BF16)
    xp = _out_proj(o.reshape(batch * s_len, d), w_out, xp)
    xs = _out_proj(_step_attention_b(qkv_s, cache_b, sink, li), w_out, xs)
    wgu, wd = w_gu_moe[li].astype(BF16), w_down_moe[li].astype(BF16)
    xp = _moe(xp, norm_ffn_moe[li], w_router[li], wgu, wd, 512)
    xs = _moe(xs, norm_ffn_moe[li], w_router[li], wgu, wd, 128)

    return (xp.reshape(batch, s_len, d), xs.reshape(n_dec, 1, d), *outs)
```
